```python
import math
import jax, jax.numpy as jnp
from jax import lax
import numpy as np

D_MODEL = 1024
BATCH = 16
SEQ = 2048
DEPTH = 2
DEC_BATCH = 32
DEC_SEQ = 8
PAST_LEN = 16384
PAGE_SIZE = 128

N_MIXERS = 2
N_RET_LAYERS = (DEPTH + 1) // 2
N_MOBA_LAYERS = DEPTH // 2
RET_HEADS = 4
RET_DK = D_MODEL // RET_HEADS
RET_DV = 2 * RET_DK
RET_HK = RET_HEADS * RET_DK
RET_HV = RET_HEADS * RET_DV
RET_CHUNK = 128
ROPE_BASE = 10000.0
MOBA_HEADS = 16
MOBA_HD = D_MODEL // MOBA_HEADS
MOBA_BLOCK = 256
MOBA_TOPK = 3
Q_BLOCK = 128
REL_BUCKETS = 32
REL_MAX_DIST = 128
D_FF = 2816
EPS = 1e-6
POOL_NUM = 5
POOL_DEN = 4

kernel_name = "retnet_moba_macaron_hybrid_step"


def rmsnorm(x, g):
    x32 = x.astype(jnp.float32)
    y = x32 * lax.rsqrt(jnp.mean(x32 * x32, axis=-1, keepdims=True) + EPS)
    return (y * g.astype(jnp.float32)).astype(x.dtype)


def swiglu(h, w_in, w_out):
    a, b = jnp.split(h @ w_in, 2, axis=-1)
    return (jax.nn.silu(a) * b) @ w_out


def rope(x, pos):
    half = x.shape[-1] // 2
    inv = ROPE_BASE ** (-jnp.arange(half, dtype=jnp.float32) / half)
    ang = pos.astype(jnp.float32)[:, None] * inv[None, :]
    cos = jnp.cos(ang)[None, :, None, :]
    sin = jnp.sin(ang)[None, :, None, :]
    x32 = x.astype(jnp.float32)
    x1, x2 = x32[..., :half], x32[..., half:]
    return jnp.concatenate([x1 * cos - x2 * sin, x1 * sin + x2 * cos], axis=-1)


def retention_chunkwise(q, k, v, s0):
    b, l, h, dk = q.shape
    dv = v.shape[-1]
    c = math.gcd(l, RET_CHUNK)
    n = l // c
    log_g = jnp.log1p(-jnp.exp2(-5.0 - jnp.arange(h, dtype=jnp.float32)))
    i = jnp.arange(c, dtype=jnp.float32)
    rel = i[:, None] - i[None, :]
    decay_in = jnp.where(rel[None] >= 0, jnp.exp(log_g[:, None, None] * jnp.maximum(rel, 0.0)[None]), 0.0)
    decay_q = jnp.exp(log_g[None, :] * (i[:, None] + 1.0))[None, :, :, None]
    decay_k = jnp.exp(log_g[None, :] * (c - 1.0 - i[:, None]))[None, :, :, None]
    decay_c = jnp.exp(log_g * c)[None, :, None, None]

    def to_chunks(t):
        return t.reshape(b, n, c, h, t.shape[-1]).transpose(1, 0, 2, 3, 4)

    def step(s, inp):
        qc, kc, vc = inp
        att = jnp.einsum('bqhd,bkhd->bhqk', qc, kc) * decay_in[None]
        o_in = jnp.einsum('bhqk,bkhe->bqhe', att, vc)
        o_x = jnp.einsum('bqhd,bhde->bqhe', qc, s) * decay_q
        s = s * decay_c + jnp.einsum('bkhd,bkhe->bhde', kc * decay_k, vc)
        return s, o_in + o_x

    s, o = lax.scan(step, s0, (to_chunks(q), to_chunks(k), to_chunks(v)))
    o = o.transpose(1, 0, 2, 3, 4).reshape(b, l, h, dv)
    return o, s


def retention_mixer(h, pos, s0, w_in, gn_gain, w_out):
    b, l, _ = h.shape
    proj = h @ w_in
    q, k, v, g = jnp.split(proj, [RET_HK, 2 * RET_HK, 2 * RET_HK + RET_HV], axis=-1)
    q = rope(q.reshape(b, l, RET_HEADS, RET_DK), pos)
    k = rope(k.reshape(b, l, RET_HEADS, RET_DK), pos) * (RET_DK ** -0.5)
    v = v.reshape(b, l, RET_HEADS, RET_DV).astype(jnp.float32)
    o, s = retention_chunkwise(q, k, v, s0.astype(jnp.float32))
    mu = jnp.mean(o, axis=-1, keepdims=True)
    var = jnp.mean(jnp.square(o - mu), axis=-1, keepdims=True)
    o = ((o - mu) * lax.rsqrt(var + EPS)).reshape(b, l, RET_HV) * gn_gain.astype(jnp.float32)
    y = (jax.nn.silu(g) * o.astype(h.dtype)) @ w_out
    return y, s.astype(h.dtype)


def t5_bucket(dist):
    n = jnp.maximum(dist, 0)
    max_exact = REL_BUCKETS // 2
    nf = jnp.maximum(n, 1).astype(jnp.float32)
    large = max_exact + (jnp.log(nf / max_exact) / math.log(REL_MAX_DIST / max_exact)
                         * (REL_BUCKETS - max_exact)).astype(jnp.int32)
    large = jnp.minimum(large, REL_BUCKETS - 1)
    return jnp.where(n < max_exact, n, large)


def moba_seq(q, q_pos, k_all, v_all, rel_bias):
    lq = q.shape[0]
    lk = k_all.shape[0]
    nb = max(-(-lk // MOBA_BLOCK), MOBA_TOPK)
    pad = nb * MOBA_BLOCK - lk
    kb = jnp.pad(k_all, ((0, pad), (0, 0), (0, 0))).reshape(nb, MOBA_BLOCK, MOBA_HEADS, MOBA_HD).transpose(2, 0, 1, 3)
    vb = jnp.pad(v_all, ((0, pad), (0, 0), (0, 0))).reshape(nb, MOBA_BLOCK, MOBA_HEADS, MOBA_HD).transpose(2, 0, 1, 3)
    kmean = jnp.mean(kb.astype(jnp.float32), axis=2).astype(q.dtype)
    qb = math.gcd(lq, Q_BLOCK)
    nq = lq // qb
    table_h = rel_bias.T
    blk = jnp.arange(nb)
    offs = jnp.arange(MOBA_BLOCK)
    rank = jnp.arange(MOBA_TOPK)

    def one_block(args):
        qq, pp = args
        own = pp // MOBA_BLOCK
        gate = jnp.einsum('qhd,hnd->hqn', qq, kmean).astype(jnp.float32)
        gate = jnp.where(blk[None, None, :] < own[None, :, None], gate, -jnp.inf)
        _, top = lax.top_k(gate, MOBA_TOPK)
        sel = jnp.concatenate([top.astype(jnp.int32), jnp.broadcast_to(own[None, :, None], (MOBA_HEADS, qb, 1)).astype(jnp.int32)], axis=-1)
        valid = jnp.concatenate([jnp.broadcast_to(rank[None, None, :] < own[None, :, None], (MOBA_HEADS, qb, MOBA_TOPK)),
                                 jnp.ones((MOBA_HEADS, qb, 1), dtype=bool)], axis=-1)
        kg = jax.vmap(lambda t, s: t[s])(kb, sel)
        vg = jax.vmap(lambda t, s: t[s])(vb, sel)
        dist = pp[None, :, None, None] - (sel[..., None] * MOBA_BLOCK + offs)
        bias = jax.vmap(lambda t, bk: t[bk])(table_h, t5_bucket(dist))
        logits = jnp.einsum('qhd,hqjkd->hqjk', qq, kg).astype(jnp.float32) * (MOBA_HD ** -0.5) + bias.astype(jnp.float32)
        logits = jnp.where(valid[..., None] & (dist >= 0), logits, -jnp.inf)
        p = jax.nn.softmax(logits.reshape(MOBA_HEADS, qb, -1), axis=-1).reshape(logits.shape)
        return jnp.einsum('hqjk,hqjkd->qhd', p.astype(vg.dtype), vg)

    out = lax.map(one_block, (q.reshape(nq, qb, MOBA_HEADS, MOBA_HD), q_pos.reshape(nq, qb)))
    return out.reshape(lq, MOBA_HEADS, MOBA_HD)


def moba_qkv(h, w_qkv):
    b, l, _ = h.shape
    q, k, v = jnp.split(h @ w_qkv, 3, axis=-1)
    shp = (b, l, MOBA_HEADS, MOBA_HD)
    return q.reshape(shp), k.reshape(shp), v.reshape(shp)


def moba_prompt(h, w_qkv, w_out, rel_bias):
    b, l, _ = h.shape
    q, k, v = moba_qkv(h, w_qkv)
    pos = jnp.arange(l, dtype=jnp.int32)
    o = lax.map(lambda a: moba_seq(a[0], pos, a[1], a[2], rel_bias), (q, k, v))
    return o.reshape(b, l, D_MODEL) @ w_out, k, v


def moba_sample(h, cache_k, cache_v, page_table, li, w_qkv, w_out, rel_bias):
    b, l, _ = h.shape
    q, k, v = moba_qkv(h, w_qkv)
    past = page_table.shape[1] * PAGE_SIZE
    pos = past + jnp.arange(l, dtype=jnp.int32)

    def one(a):
        qq, kk, vv, pt = a
        kp = cache_k[li, pt].reshape(past, MOBA_HEADS, MOBA_HD)
        vp = cache_v[li, pt].reshape(past, MOBA_HEADS, MOBA_HD)
        k_all = jnp.concatenate([kp.astype(kk.dtype), kk], axis=0)
        v_all = jnp.concatenate([vp.astype(vv.dtype), vv], axis=0)
        return moba_seq(qq, pos, k_all, v_all, rel_bias)

    o = lax.map(one, (q, k, v, page_table))
    return o.reshape(b, l, D_MODEL) @ w_out, k, v


def setup_inputs(seed: int = 0) -> dict:
    key = jax.random.key(seed)
    ks = jax.random.split(key, 24)
    f32 = jnp.float32
    n_pages = PAST_LEN // PAGE_SIZE
    n_pool = (DEC_BATCH * n_pages * POOL_NUM) // POOL_DEN

    def w(k, shape, fan_in):
        return jax.random.normal(k, shape, f32) * (fan_in ** -0.5)

    def gain(k, shape):
        return 1.0 + 0.01 * jax.random.normal(k, shape, f32)

    page_table = jax.random.permutation(ks[5], n_pool)[: DEC_BATCH * n_pages].reshape(DEC_BATCH, n_pages).astype(jnp.int32)
    return {
        "x_prompt": jax.random.normal(ks[0], (BATCH, SEQ, D_MODEL), f32),
        "x_sample": jax.random.normal(ks[1], (DEC_BATCH, DEC_SEQ, D_MODEL), f32),
        "state_ret": 0.5 * jax.random.normal(ks[2], (N_RET_LAYERS, DEC_BATCH, RET_HEADS, RET_DK, RET_DV), f32),
        "cache_k": jax.random.normal(ks[3], (N_MOBA_LAYERS, n_pool, PAGE_SIZE, MOBA_HEADS, MOBA_HD), f32),
        "cache_v": jax.random.normal(ks[4], (N_MOBA_LAYERS, n_pool, PAGE_SIZE, MOBA_HEADS, MOBA_HD), f32),
        "page_table": page_table,
        "ffn1_norm": gain(ks[6], (DEPTH, D_MODEL)),
        "ffn1_w_in": w(ks[7], (DEPTH, D_MODEL, 2 * D_FF), D_MODEL),
        "ffn1_w_out": w(ks[8], (DEPTH, D_FF, D_MODEL), D_FF),
        "mix_norm": gain(ks[9], (DEPTH, D_MODEL)),
        "ffn2_norm": gain(ks[10], (DEPTH, D_MODEL)),
        "ffn2_w_in": w(ks[11], (DEPTH, D_MODEL, 2 * D_FF), D_MODEL),
        "ffn2_w_out": w(ks[12], (DEPTH, D_FF, D_MODEL), D_FF),
        "ret_w_in": w(ks[13], (N_RET_LAYERS, D_MODEL, 2 * RET_HK + 2 * RET_HV), D_MODEL),
        "ret_gn_gain": gain(ks[14], (N_RET_LAYERS, RET_HV)),
        "ret_w_out": w(ks[15], (N_RET_LAYERS, RET_HV, D_MODEL), RET_HV),
        "moba_w_qkv": w(ks[16], (N_MOBA_LAYERS, D_MODEL, 3 * D_MODEL), D_MODEL),
        "moba_w_out": w(ks[17], (N_MOBA_LAYERS, D_MODEL, D_MODEL), D_MODEL),
        "rel_bias": 0.5 * jax.random.normal(ks[18], (REL_BUCKETS, MOBA_HEADS), f32),
        "final_norm": gain(ks[19], (D_MODEL,)),
    }


def reference(x_prompt, x_sample, state_ret, cache_k, cache_v, page_table,
              ffn1_norm, ffn1_w_in, ffn1_w_out, mix_norm, ffn2_norm, ffn2_w_in, ffn2_w_out,
              ret_w_in, ret_gn_gain, ret_w_out, moba_w_qkv, moba_w_out, rel_bias, final_norm):
    bp, lp, _ = x_prompt.shape
    ls = x_sample.shape[1]
    pos_p = jnp.arange(lp, dtype=jnp.int32)
    pos_s = PAST_LEN + jnp.arange(ls, dtype=jnp.int32)
    hp, hs = x_prompt, x_sample
    ret_p, ret_s, kp_l, vp_l, ks_l, vs_l = [], [], [], [], [], []
    for i in range(DEPTH):
        hp = hp + 0.5 * swiglu(rmsnorm(hp, ffn1_norm[i]), ffn1_w_in[i], ffn1_w_out[i])
        hs = hs + 0.5 * swiglu(rmsnorm(hs, ffn1_norm[i]), ffn1_w_in[i], ffn1_w_out[i])
        np_ = rmsnorm(hp, mix_norm[i])
        ns_ = rmsnorm(hs, mix_norm[i])
        if i % N_MIXERS == 0:
            r = i // N_MIXERS
            s0 = jnp.zeros((bp, RET_HEADS, RET_DK, RET_DV), x_prompt.dtype)
            yp, sp = retention_mixer(np_, pos_p, s0, ret_w_in[r], ret_gn_gain[r], ret_w_out[r])
            ys, ss = retention_mixer(ns_, pos_s, state_ret[r], ret_w_in[r], ret_gn_gain[r], ret_w_out[r])
            ret_p.append(sp)
            ret_s.append(ss)
        else:
            m = i // N_MIXERS
            yp, kp, vp = moba_prompt(np_, moba_w_qkv[m], moba_w_out[m], rel_bias)
            ys, ksn, vsn = moba_sample(ns_, cache_k, cache_v, page_table, m, moba_w_qkv[m], moba_w_out[m], rel_bias)
            kp_l.append(kp)
            vp_l.append(vp)
            ks_l.append(ksn)
            vs_l.append(vsn)
        hp = hp + yp
        hs = hs + ys
        hp = hp + 0.5 * swiglu(rmsnorm(hp, ffn2_norm[i]), ffn2_w_in[i], ffn2_w_out[i])
        hs = hs + 0.5 * swiglu(rmsnorm(hs, ffn2_norm[i]), ffn2_w_in[i], ffn2_w_out[i])
    y_prompt = rmsnorm(hp, final_norm)
    y_sample = rmsnorm(hs, final_norm)
    return (y_prompt, y_sample, jnp.stack(ret_p), jnp.stack(ret_s), jnp.stack(kp_l), jnp.stack(vp_l), jnp.stack(ks_l), jnp.stack(vs_l))
```

```python
import functools
import math

import jax
import jax.numpy as jnp
import numpy as np
from jax import lax
from jax.experimental import pallas as pl
from jax.experimental.pallas import tpu as pltpu

F32 = jnp.float32
BF16 = jnp.bfloat16
NEG_INF = float("-inf")

LANES = 128
SUBLANES = 8
VMEM_LIMIT_BYTES = 56 * 1024 * 1024

EPS = 1e-6
ROPE_BASE = 10000.0
RET_HEADS = 4
RET_CHUNK = 128
MOBA_HEADS = 16
MOBA_BLOCK = 256
MOBA_TOPK = 3
PAGE_SIZE = 128
REL_BUCKETS = 32
REL_MAX_DIST = 128


def _params(*semantics):
    return pltpu.CompilerParams(dimension_semantics=semantics, vmem_limit_bytes=VMEM_LIMIT_BYTES)


def _resident(shape):
    zeros = (0,) * len(shape)
    return pl.BlockSpec(shape, lambda *_: zeros, pipeline_mode=pl.Buffered(1))


def _row_tile(n, want):
    t = min(n, want)
    assert n % t == 0, (n, t)
    return t


def _rms(x, g):
    y = x * lax.rsqrt(jnp.mean(x * x, axis=-1, keepdims=True) + EPS)
    return y * g


def _ffn_body(*refs, d_ff, tf, final):
    if final:
        x_ref, g_ref, win_ref, wout_ref, fg_ref, o_ref = refs
    else:
        x_ref, g_ref, win_ref, wout_ref, o_ref = refs
    x = x_ref[...]
    h = _rms(x, g_ref[...]).astype(BF16)
    y = None
    for c in range(d_ff // tf):
        a = jnp.dot(h, win_ref[:, c * tf:(c + 1) * tf], preferred_element_type=F32)
        b = jnp.dot(h, win_ref[:, d_ff + c * tf:d_ff + (c + 1) * tf], preferred_element_type=F32)
        gated = (a * jax.nn.sigmoid(a) * b).astype(BF16)
        part = jnp.dot(gated, wout_ref[c * tf:(c + 1) * tf, :], preferred_element_type=F32)
        y = part if y is None else y + part
    out = x + 0.5 * y
    if final:
        out = _rms(out, fg_ref[...])
    o_ref[...] = out


def _ffn(x, g, w_in, w_out, final_g=None, *, tm=512, tf=1408):
    n, d = x.shape
    d_ff = w_out.shape[0]
    tm = _row_tile(n, tm)
    assert d_ff % tf == 0 and tf % LANES == 0
    final = final_g is not None
    ins = [x, g.reshape(1, d), w_in, w_out]
    specs = [pl.BlockSpec((tm, d), lambda i: (i, 0)), _resident((1, d)),
             _resident(w_in.shape), _resident(w_out.shape)]
    if final:
        ins.append(final_g.reshape(1, d))
        specs.append(_resident((1, d)))
    return pl.pallas_call(
        functools.partial(_ffn_body, d_ff=d_ff, tf=tf, final=final),
        grid=(n // tm,),
        in_specs=specs,
        out_specs=pl.BlockSpec((tm, d), lambda i: (i, 0)),
        out_shape=jax.ShapeDtypeStruct((n, d), F32),
        compiler_params=_params("parallel"),
        name="ffn",
    )(*ins)


def _norm_proj_body(x_ref, g_ref, w_ref, o_ref):
    h = _rms(x_ref[...], g_ref[...]).astype(BF16)
    o_ref[...] = jnp.dot(h, w_ref[...], preferred_element_type=F32)


def _norm_proj(x, g, w, *, tm=256):
    n, d = x.shape
    nout = w.shape[1]
    tm = _row_tile(n, tm)
    return pl.pallas_call(
        _norm_proj_body,
        grid=(n // tm,),
        in_specs=[pl.BlockSpec((tm, d), lambda i: (i, 0)), _resident((1, d)), _resident(w.shape)],
        out_specs=pl.BlockSpec((tm, nout), lambda i: (i, 0)),
        out_shape=jax.ShapeDtypeStruct((n, nout), F32),
        compiler_params=_params("parallel"),
        name="norm_proj",
    )(x, g.reshape(1, d), w)


def _proj_res_body(x_ref, w_ref, r_ref, o_ref):
    o_ref[...] = r_ref[...] + jnp.dot(x_ref[...], w_ref[...], preferred_element_type=F32)


def _proj_res(x, w, res, *, tm=512):
    n, k = x.shape
    d = w.shape[1]
    tm = _row_tile(n, tm)
    return pl.pallas_call(
        _proj_res_body,
        grid=(n // tm,),
        in_specs=[pl.BlockSpec((tm, k), lambda i: (i, 0)), _resident(w.shape),
                  pl.BlockSpec((tm, d), lambda i: (i, 0))],
        out_specs=pl.BlockSpec((tm, d), lambda i: (i, 0)),
        out_shape=jax.ShapeDtypeStruct((n, d), F32),
        compiler_params=_params("parallel"),
        name="proj_res",
    )(x, w, res)


def _ret_body(*refs, nh, dk, dv, c, cp, has_s0):
    if has_s0:
        (q_ref, k_ref, v_ref, g_ref, cos_ref, sin_ref, din_ref, dq_ref, dkk_ref, dc_ref,
         gain_ref, s0_ref, o_ref, s_ref) = refs
    else:
        (q_ref, k_ref, v_ref, g_ref, cos_ref, sin_ref, din_ref, dq_ref, dkk_ref, dc_ref,
         gain_ref, o_ref, s_ref) = refs
    i = pl.program_id(1)

    @pl.when(i == 0)
    def _():
        if has_s0:
            s_ref[...] = s0_ref[...]
        else:
            s_ref[...] = jnp.zeros(s_ref.shape, F32)

    def pad(t):
        if cp == c:
            return t
        return jnp.concatenate([t, jnp.zeros((cp - c, t.shape[1]), t.dtype)], axis=0)

    cos = pad(cos_ref[...])
    sin = pad(sin_ref[...])
    half = dk // 2

    def rope(t):
        t1, t2 = t[:, :half], t[:, half:]
        return jnp.concatenate([t1 * cos - t2 * sin, t1 * sin + t2 * cos], axis=-1)

    for h in range(nh):
        qr = rope(pad(q_ref[0, :, h * dk:(h + 1) * dk]))
        kr = rope(pad(k_ref[0, :, h * dk:(h + 1) * dk])) * (dk ** -0.5)
        qb = qr.astype(BF16)
        kb = kr.astype(BF16)
        vb = pad(v_ref[0, :, h * dv:(h + 1) * dv]).astype(BF16)
        att = lax.dot_general(qb, kb, (((1,), (1,)), ((), ())), preferred_element_type=F32) * din_ref[h]
        o_in = jnp.dot(att.astype(BF16), vb, preferred_element_type=F32)
        s = s_ref[0, h]
        o_x = jnp.dot(qb, s.astype(BF16), preferred_element_type=F32) * dq_ref[h]
        kd = (kr * dkk_ref[h]).T.astype(BF16)
        s_ref[0, h] = s * dc_ref[h] + jnp.dot(kd, vb, preferred_element_type=F32)
        o = (o_in + o_x)[:c]
        mu = jnp.mean(o, axis=-1, keepdims=True)
        d = o - mu
        var = jnp.mean(d * d, axis=-1, keepdims=True)
        on = d * lax.rsqrt(var + EPS) * gain_ref[:, h * dv:(h + 1) * dv]
        gt = g_ref[0, :, h * dv:(h + 1) * dv]
        o_ref[0, :, h * dv:(h + 1) * dv] = (gt * jax.nn.sigmoid(gt) * on).astype(BF16)


def _retention(proj, pos, gn_gain, s0):
    b, l, width = proj.shape
    nh = RET_HEADS
    hk = width // 6
    hv = 2 * hk
    dk, dv = hk // nh, hv // nh
    c = math.gcd(l, RET_CHUNK)
    cp = RET_CHUNK
    n = l // c
    half = dk // 2
    inv = ROPE_BASE ** (-jnp.arange(half, dtype=F32) / half)
    ang = pos.astype(F32)[:, None] * inv[None, :]
    cos, sin = jnp.cos(ang), jnp.sin(ang)
    log_g = jnp.log1p(-jnp.exp2(-5.0 - jnp.arange(nh, dtype=F32)))
    idx = jnp.arange(c, dtype=F32)
    rel = idx[:, None] - idx[None, :]
    d_in = jnp.where(rel[None] >= 0, jnp.exp(log_g[:, None, None] * jnp.maximum(rel, 0.0)[None]), 0.0)
    d_q = jnp.exp(log_g[:, None] * (idx[None, :] + 1.0))
    d_k = jnp.exp(log_g[:, None] * (c - 1.0 - idx[None, :]))
    d_c = jnp.exp(log_g * c)
    d_in = jnp.pad(d_in, ((0, 0), (0, cp - c), (0, cp - c)))
    d_q = jnp.pad(d_q, ((0, 0), (0, cp - c)))[:, :, None]
    d_k = jnp.pad(d_k, ((0, 0), (0, cp - c)))[:, :, None]
    d_c = d_c[:, None, None]

    has_s0 = s0 is not None
    ins = [proj, proj, proj, proj, cos, sin, d_in, d_q, d_k, d_c, gn_gain.reshape(1, hv)]
    specs = [
        pl.BlockSpec((1, c, hk), lambda bi, i: (bi, i, 0)),
        pl.BlockSpec((1, c, hk), lambda bi, i: (bi, i, 1)),
        pl.BlockSpec((1, c, hv), lambda bi, i: (bi, i, 1)),
        pl.BlockSpec((1, c, hv), lambda bi, i: (bi, i, 2)),
        pl.BlockSpec((c, half), lambda bi, i: (i, 0)),
        pl.BlockSpec((c, half), lambda bi, i: (i, 0)),
        _resident(d_in.shape), _resident(d_q.shape), _resident(d_k.shape), _resident(d_c.shape),
        _resident((1, hv)),
    ]
    if has_s0:
        ins.append(s0)
        specs.append(pl.BlockSpec((1, nh, dk, dv), lambda bi, i: (bi, 0, 0, 0)))
    return pl.pallas_call(
        functools.partial(_ret_body, nh=nh, dk=dk, dv=dv, c=c, cp=cp, has_s0=has_s0),
        grid=(b, n),
        in_specs=specs,
        out_specs=[pl.BlockSpec((1, c, hv), lambda bi, i: (bi, i, 0)),
                   pl.BlockSpec((1, nh, dk, dv), lambda bi, i: (bi, 0, 0, 0))],
        out_shape=[jax.ShapeDtypeStruct((b, l, hv), BF16),
                   jax.ShapeDtypeStruct((b, nh, dk, dv), F32)],
        compiler_params=_params("parallel", "arbitrary"),
        name="retention",
    )(*ins)


def _qkv_body(x_ref, g_ref, w_ref, q_ref, k_ref, v_ref, kb_ref, vt_ref, km_ref, *, d, blk, scale):
    h = _rms(x_ref[...], g_ref[...]).astype(BF16)
    y = jnp.dot(h, w_ref[...], preferred_element_type=F32)
    k = y[:, d:2 * d]
    v = y[:, 2 * d:]
    q_ref[...] = (y[:, :d] * scale).astype(BF16)
    k_ref[...] = k
    v_ref[...] = v
    kb_ref[...] = k.astype(BF16)
    vt_ref[0] = v.T.astype(BF16)
    for r in range(k.shape[0] // blk):
        km_ref[r] = jnp.sum(k[r * blk:(r + 1) * blk], axis=0, keepdims=True) * (1.0 / blk)


def _moba_qkv(x, g, w, seq, *, tm=256):
    n, d = x.shape
    hd = d // MOBA_HEADS
    blk = MOBA_BLOCK
    tm = _row_tile(n, tm)
    assert tm % blk == 0 and seq % tm == 0
    per_seq = seq // tm
    row = lambda i: (i, 0)
    return pl.pallas_call(
        functools.partial(_qkv_body, d=d, blk=blk, scale=hd ** -0.5),
        grid=(n // tm,),
        in_specs=[pl.BlockSpec((tm, d), row), _resident((1, d)), _resident(w.shape)],
        out_specs=[pl.BlockSpec((tm, d), row), pl.BlockSpec((tm, d), row), pl.BlockSpec((tm, d), row),
                   pl.BlockSpec((tm, d), row),
                   pl.BlockSpec((1, d, tm), lambda i: (i // per_seq, 0, i % per_seq)),
                   pl.BlockSpec((tm // blk, 1, d), lambda i: (i, 0, 0))],
        out_shape=[jax.ShapeDtypeStruct((n, d), BF16), jax.ShapeDtypeStruct((n, d), F32),
                   jax.ShapeDtypeStruct((n, d), F32), jax.ShapeDtypeStruct((n, d), BF16),
                   jax.ShapeDtypeStruct((n // seq, d, seq), BF16),
                   jax.ShapeDtypeStruct((n // blk, 1, d), F32)],
        compiler_params=_params("parallel"),
        name="moba_qkv",
    )(x, g.reshape(1, d), w)


def _qkv_small_body(x_ref, g_ref, w_ref, q_ref, k_ref, v_ref, *, d, scale):
    h = _rms(x_ref[...], g_ref[...]).astype(BF16)
    y = jnp.dot(h, w_ref[...], preferred_element_type=F32)
    q_ref[...] = y[:, :d] * scale
    k_ref[...] = y[:, d:2 * d]
    v_ref[...] = y[:, 2 * d:]


def _moba_qkv_small(x, g, w):
    n, d = x.shape
    hd = d // MOBA_HEADS
    full = pl.BlockSpec((n, d), lambda i: (0, 0))
    return pl.pallas_call(
        functools.partial(_qkv_small_body, d=d, scale=hd ** -0.5),
        grid=(1,),
        in_specs=[full, _resident((1, d)), _resident(w.shape)],
        out_specs=[full, full, full],
        out_shape=[jax.ShapeDtypeStruct((n, d), F32)] * 3,
        compiler_params=_params("arbitrary"),
        name="moba_qkv_small",
    )(x, g.reshape(1, d), w)


def _t5_bucket_np(dist):
    n = np.maximum(dist, 0)
    max_exact = REL_BUCKETS // 2
    nf = np.maximum(n, 1).astype(np.float64)
    large = max_exact + (np.log(nf / max_exact) / math.log(REL_MAX_DIST / max_exact)
                         * (REL_BUCKETS - max_exact)).astype(np.int64)
    large = np.minimum(large, REL_BUCKETS - 1)
    return np.where(n < max_exact, n, large).astype(np.int32)


def _bias_body(bk_ref, tab_ref, o_ref):
    bk = bk_ref[0]
    acc = jnp.where(bk < 0, NEG_INF, 0.0).astype(F32)
    for b in range(REL_BUCKETS):
        acc = jnp.where(bk == b, tab_ref[0, b:b + 1, :], acc)
    o_ref[0, 0] = acc


def _bias_tiles(buckets, table):
    t, r, w = buckets.shape
    g = table.shape[0]
    return pl.pallas_call(
        _bias_body,
        grid=(g, t),
        in_specs=[pl.BlockSpec((1, r, w), lambda gi, ti: (ti, 0, 0)),
                  pl.BlockSpec((1, REL_BUCKETS, w), lambda gi, ti: (gi, 0, 0))],
        out_specs=pl.BlockSpec((1, 1, r, w), lambda gi, ti: (gi, ti, 0, 0)),
        out_shape=jax.ShapeDtypeStruct((g, t, r, w), F32),
        compiler_params=_params("parallel", "parallel"),
        name="t5_bias_tiles",
    )(jnp.asarray(buckets), table)


def _flash_body(q_ref, k_ref, vt_ref, km_ref, bias_ref, o_ref, neg_ref, *, nb, blk, hd, topk):
    qi = pl.program_id(2)
    q = q_ref[0]
    pair = 2 * hd
    lane = lax.broadcasted_iota(jnp.int32, (1, pair), 1)
    row = lax.broadcasted_iota(jnp.int32, (pair, 1), 0)
    km = km_ref[0].astype(BF16)
    bidx = lax.broadcasted_iota(jnp.int32, (nb, 1), 0)
    nt = (((1,), (1,)), ((), ()))
    out_t = None
    for hh in range(2):
        qm = jnp.where((lane >= hd) == bool(hh), q, jnp.zeros_like(q))
        gate = lax.dot_general(km, qm, nt, preferred_element_type=F32)
        valid = bidx < qi
        gate = jnp.where(valid, gate, NEG_INF)
        rank = jnp.zeros((nb, blk), jnp.int32)
        for j2 in range(nb):
            r2 = gate[j2:j2 + 1, :]
            ahead = (r2 > gate) | ((r2 == gate) & (j2 < bidx))
            rank = rank + ahead.astype(jnp.int32)
        neg_ref[...] = jnp.where(valid & (rank < topk), 0.0, NEG_INF).astype(F32)

        start = pl.multiple_of(qi * blk, blk)
        s = lax.dot_general(k_ref[0, pl.ds(start, blk), :], qm, nt, preferred_element_type=F32)
        s = s + bias_ref[hh, 0]
        m = jnp.max(s, axis=0, keepdims=True)
        p = jnp.exp(s - m)
        l = jnp.sum(p, axis=0, keepdims=True)
        acc = jnp.dot(vt_ref[0, :, pl.ds(start, blk)], p.astype(BF16), preferred_element_type=F32)

        def body(j, carry):
            m, l, acc = carry
            st = pl.multiple_of(j * blk, blk)
            s = lax.dot_general(k_ref[0, pl.ds(st, blk), :], qm, nt, preferred_element_type=F32)
            s = s + bias_ref[hh, jnp.minimum(qi - j, 2)] + neg_ref[pl.ds(j, 1), :]
            m_new = jnp.maximum(m, jnp.max(s, axis=0, keepdims=True))
            alpha = jnp.exp(m - m_new)
            p = jnp.exp(s - m_new)
            l = alpha * l + jnp.sum(p, axis=0, keepdims=True)
            acc = alpha * acc + jnp.dot(vt_ref[0, :, pl.ds(st, blk)], p.astype(BF16),
                                        preferred_element_type=F32)
            return m_new, l, acc

        m, l, acc = lax.fori_loop(0, qi, body, (m, l, acc))
        res = acc / l
        out_t = res if out_t is None else jnp.where(row >= hd, res, out_t)
    o_ref[0] = out_t.T.astype(BF16)


def _moba_flash(q, kb, vt, kmean, bias):
    b, l, d = q.shape
    hd = d // MOBA_HEADS
    pair = 2 * hd
    assert pair == LANES
    blk = MOBA_BLOCK
    nb = l // blk
    return pl.pallas_call(
        functools.partial(_flash_body, nb=nb, blk=blk, hd=hd, topk=MOBA_TOPK),
        grid=(b, d // pair, nb),
        in_specs=[pl.BlockSpec((1, blk, pair), lambda bi, p, qi: (bi, qi, p)),
                  pl.BlockSpec((1, l, pair), lambda bi, p, qi: (bi, 0, p)),
                  pl.BlockSpec((1, pair, l), lambda bi, p, qi: (bi, p, 0)),
                  pl.BlockSpec((1, nb, pair), lambda bi, p, qi: (bi, 0, p)),
                  pl.BlockSpec((2, 3, blk, blk), lambda bi, p, qi: (p, 0, 0, 0))],
        out_specs=pl.BlockSpec((1, blk, pair), lambda bi, p, qi: (bi, qi, p)),
        out_shape=jax.ShapeDtypeStruct((b, l, d), BF16),
        scratch_shapes=[pltpu.VMEM((nb, blk), F32)],
        compiler_params=_params("parallel", "parallel", "arbitrary"),
        name="moba_flash",
    )(q, kb, vt, kmean, bias)


def _prompt_bias_buckets(blk):
    assert REL_MAX_DIST <= blk + 1
    key = np.arange(blk)[:, None]
    qry = np.arange(blk)[None, :]
    tiles = []
    for dblk in range(3):
        dist = dblk * blk + qry - key
        tiles.append(np.where(dist >= 0, _t5_bucket_np(dist), -1))
    return np.stack(tiles).astype(np.int32)


def _sample_body(pt_ref, bsel_ref, qbd_ref, ka_ref, kb_ref, va_ref, vb_ref, kn_ref, vn_ref, bias_ref,
                 o_ref, m_ref, l_ref, km_ref, acc_ref, *, nbp, blk, heads, lq, topk):
    j = pl.program_id(1)
    hq = heads * lq
    qbd = qbd_ref[0]
    d = qbd.shape[0]
    hd = d // heads

    @pl.when(j == 0)
    def _():
        m_ref[...] = jnp.zeros(m_ref.shape, F32)
        l_ref[...] = jnp.zeros(l_ref.shape, F32)
        km_ref[...] = jnp.zeros(km_ref.shape, F32)

    def block(k, v, bias):
        s = jnp.dot(k.astype(BF16), qbd, preferred_element_type=F32) + bias
        m = jnp.max(s, axis=0, keepdims=True)
        p = jnp.exp(s - m)
        l = jnp.sum(p, axis=0, keepdims=True)
        acc = jnp.dot(p.T.astype(BF16), v.astype(BF16), preferred_element_type=F32)
        slab = jnp.concatenate(
            [acc[h * lq:(h + 1) * lq, (h * hd // LANES) * LANES:(h * hd // LANES + 1) * LANES]
             for h in range(heads)], axis=0)
        m_ref[pl.ds(j, 1), :] = m
        l_ref[pl.ds(j, 1), :] = l
        acc_ref[j] = slab

    @pl.when(j < nbp)
    def _():
        k = jnp.concatenate([ka_ref[0], kb_ref[0]], axis=0)
        v = jnp.concatenate([va_ref[0], vb_ref[0]], axis=0)
        km_ref[pl.ds(j, 1), :] = jnp.sum(k, axis=0, keepdims=True) * (1.0 / blk)
        block(k, v, bias_ref[bsel_ref[j]])

    @pl.when(j == nbp)
    def _():
        block(kn_ref[0], vn_ref[0], bias_ref[bsel_ref[j]])
        rows = m_ref.shape[0]
        ridx = lax.broadcasted_iota(jnp.int32, (rows, 1), 0)
        gate = jnp.dot(km_ref[...].astype(BF16), qbd, preferred_element_type=F32)
        gate = jnp.where(ridx < nbp, gate, NEG_INF)
        sel = ridx == nbp
        for _ in range(min(topk, nbp)):
            best = jnp.max(gate, axis=0, keepdims=True)
            first = jnp.min(jnp.where(gate == best, ridx, rows), axis=0, keepdims=True)
            pick = ridx == first
            sel = sel | pick
            gate = jnp.where(pick, NEG_INF, gate)
        m_all = m_ref[...]
        top = jnp.max(jnp.where(sel, m_all, NEG_INF), axis=0, keepdims=True)
        w = jnp.where(sel, jnp.exp(m_all - top), 0.0)
        den = jnp.sum(w * l_ref[...], axis=0, keepdims=True)
        wt = (w / den).T
        out = jnp.zeros((hq, LANES), F32)
        for jj in range(nbp + 1):
            out = out + wt[:, jj:jj + 1] * acc_ref[jj]
        lane = lax.broadcasted_iota(jnp.int32, (1, LANES), 1)
        rrow = lax.broadcasted_iota(jnp.int32, (hq, 1), 0)
        keep = (lane // hd) == ((rrow // lq) % (LANES // hd))
        o_ref[0] = jnp.where(keep, out, 0.0)


def _moba_sample(q, k_new, v_new, cache_k, cache_v, page_base, page_table, rel_bias):
    b, lq, d = q.shape
    heads = MOBA_HEADS
    hd = d // heads
    hq = heads * lq
    blk = MOBA_BLOCK
    n_pages = page_table.shape[1]
    past = n_pages * PAGE_SIZE
    assert hq == LANES and blk == 2 * PAGE_SIZE and past % blk == 0 and lq <= blk
    nbp = past // blk
    rows = LANES
    assert nbp + 1 <= rows

    qh = q.reshape(b, lq, heads, hd).transpose(0, 2, 3, 1)
    eye = jnp.eye(heads, dtype=F32)
    qbd = (qh[:, :, :, None, :] * eye[None, :, None, :, None]).reshape(b, d, hq).astype(BF16)

    key = np.arange(blk)[:, None]
    qry = np.tile(np.arange(lq), heads)[None, :]
    tiles, bsel = [], []
    for jb in range(nbp + 1):
        dist = past + qry - (jb * blk + key)
        bk = np.where(dist >= 0, _t5_bucket_np(dist), -1).astype(np.int32)
        for t, existing in enumerate(tiles):
            if np.array_equal(existing, bk):
                bsel.append(t)
                break
        else:
            tiles.append(bk)
            bsel.append(len(tiles) - 1)
    table = jnp.repeat(rel_bias, lq, axis=1)[None]
    bias = _bias_tiles(np.stack(tiles), table)[0]
    nt = bias.shape[0]

    kn = jnp.pad(k_new, ((0, 0), (0, blk - lq), (0, 0)))
    vn = jnp.pad(v_new, ((0, 0), (0, blk - lq), (0, 0)))
    last = nbp - 1

    def page(off):
        return lambda bi, j, pt, bs: (page_base + pt[bi * n_pages + 2 * jnp.minimum(j, last) + off], 0, 0)

    per_b = lambda bi, j, pt, bs: (bi, 0, 0)
    grid_spec = pltpu.PrefetchScalarGridSpec(
        num_scalar_prefetch=2,
        grid=(b, nbp + 1),
        in_specs=[pl.BlockSpec((1, d, hq), per_b),
                  pl.BlockSpec((1, PAGE_SIZE, d), page(0)), pl.BlockSpec((1, PAGE_SIZE, d), page(1)),
                  pl.BlockSpec((1, PAGE_SIZE, d), page(0)), pl.BlockSpec((1, PAGE_SIZE, d), page(1)),
                  pl.BlockSpec((1, blk, d), per_b), pl.BlockSpec((1, blk, d), per_b),
                  pl.BlockSpec((nt, blk, hq), lambda bi, j, pt, bs: (0, 0, 0))],
        out_specs=pl.BlockSpec((1, hq, LANES), per_b),
        scratch_shapes=[pltpu.VMEM((rows, hq), F32), pltpu.VMEM((rows, hq), F32),
                        pltpu.VMEM((rows, d), F32), pltpu.VMEM((nbp + 1, hq, LANES), F32)],
    )
    o2 = pl.pallas_call(
        functools.partial(_sample_body, nbp=nbp, blk=blk, heads=heads, lq=lq, topk=MOBA_TOPK),
        grid_spec=grid_spec,
        out_shape=jax.ShapeDtypeStruct((b, hq, LANES), F32),
        compiler_params=_params("parallel", "arbitrary"),
        name="moba_sample",
    )(page_table.reshape(-1), jnp.asarray(np.array(bsel, np.int32)), qbd,
      cache_k, cache_k, cache_v, cache_v, kn, vn, bias)
    o = o2.reshape(b, heads, lq, LANES // hd, hd).sum(axis=3)
    return o.transpose(0, 2, 1, 3).reshape(b, lq, d).astype(BF16)


def kernel(x_prompt, x_sample, state_ret, cache_k, cache_v, page_table, ffn1_norm, ffn1_w_in, ffn1_w_out,
           mix_norm, ffn2_norm, ffn2_w_in, ffn2_w_out, ret_w_in, ret_gn_gain, ret_w_out, moba_w_qkv,
           moba_w_out, rel_bias, final_norm):
    bp, lp, d = x_prompt.shape
    bs, ls, _ = x_sample.shape
    depth = ffn1_norm.shape[0]
    past = page_table.shape[1] * PAGE_SIZE
    pos_p = jnp.arange(lp, dtype=jnp.int32)
    pos_s = past + jnp.arange(ls, dtype=jnp.int32)
    hp = x_prompt.reshape(bp * lp, d)
    hs = x_sample.reshape(bs * ls, d)
    cast = lambda w: w.astype(BF16)
    ret_p, ret_s, kp_l, vp_l, ks_l, vs_l = [], [], [], [], [], []
    for i in range(depth):
        w1i, w1o = cast(ffn1_w_in[i]), cast(ffn1_w_out[i])
        hp = _ffn(hp, ffn1_norm[i], w1i, w1o)
        hs = _ffn(hs, ffn1_norm[i], w1i, w1o)
        if i % 2 == 0:
            r = i // 2
            w_in, w_out = cast(ret_w_in[r]), cast(ret_w_out[r])
            proj_p = _norm_proj(hp, mix_norm[i], w_in).reshape(bp, lp, -1)
            proj_s = _norm_proj(hs, mix_norm[i], w_in).reshape(bs, ls, -1)
            gp, sp = _retention(proj_p, pos_p, ret_gn_gain[r], None)
            gs, ss = _retention(proj_s, pos_s, ret_gn_gain[r], state_ret[r])
            hp = _proj_res(gp.reshape(bp * lp, -1), w_out, hp)
            hs = _proj_res(gs.reshape(bs * ls, -1), w_out, hs)
            ret_p.append(sp)
            ret_s.append(ss)
        else:
            m = i // 2
            w_qkv, w_out = cast(moba_w_qkv[m]), cast(moba_w_out[m])
            hd = d // MOBA_HEADS
            q, k, v, kb, vt, km = _moba_qkv(hp, mix_norm[i], w_qkv, lp)
            table = jnp.broadcast_to(rel_bias.T[:, :, None], (MOBA_HEADS, REL_BUCKETS, MOBA_BLOCK))
            bias = _bias_tiles(_prompt_bias_buckets(MOBA_BLOCK), table)
            op = _moba_flash(q.reshape(bp, lp, d), kb.reshape(bp, lp, d), vt,
                             km.reshape(bp, lp // MOBA_BLOCK, d), bias)
            hp = _proj_res(op.reshape(bp * lp, d), w_out, hp)
            kp_l.append(k.reshape(bp, lp, MOBA_HEADS, hd))
            vp_l.append(v.reshape(bp, lp, MOBA_HEADS, hd))

            qs, ksn, vsn = _moba_qkv_small(hs, mix_norm[i], w_qkv)
            n_pool = cache_k.shape[1]
            os_ = _moba_sample(qs.reshape(bs, ls, d), ksn.reshape(bs, ls, d), vsn.reshape(bs, ls, d),
                               cache_k.reshape(-1, PAGE_SIZE, d), cache_v.reshape(-1, PAGE_SIZE, d),
                               m * n_pool, page_table, rel_bias)
            hs = _proj_res(os_.reshape(bs * ls, d), w_out, hs)
            ks_l.append(ksn.reshape(bs, ls, MOBA_HEADS, hd))
            vs_l.append(vsn.reshape(bs, ls, MOBA_HEADS, hd))
        last = i == depth - 1
        w2i, w2o = cast(ffn2_w_in[i]), cast(ffn2_w_out[i])
        hp = _ffn(hp, ffn2_norm[i], w2i, w2o, final_norm if last else None)
        hs = _ffn(hs, ffn2_norm[i], w2i, w2o, final_norm if last else None)
    return (hp.reshape(bp, lp, d), hs.reshape(bs, ls, d), jnp.stack(ret_p), jnp.stack(ret_s),
            jnp.stack(kp_l), jnp.stack(vp_l), jnp.stack(ks_l), jnp.stack(vs_l))
```

```python
import functools
import math

import jax
import jax.numpy as jnp
import numpy as np
from jax import lax
from jax.experimental import pallas as pl
from jax.experimental.pallas import tpu as pltpu

F32 = jnp.float32
BF16 = jnp.bfloat16
NEG_INF = float("-inf")

LANES = 128
SUBLANES = 8
VMEM_LIMIT_BYTES = 56 * 1024 * 1024

EPS = 1e-6
ROPE_BASE = 10000.0
RET_HEADS = 4
RET_CHUNK = 128
MOBA_HEADS = 16
MOBA_BLOCK = 256
MOBA_TOPK = 3
PAGE_SIZE = 128
REL_BUCKETS = 32
REL_MAX_DIST = 128


def _params(*semantics):
    return pltpu.CompilerParams(dimension_semantics=semantics, vmem_limit_bytes=VMEM_LIMIT_BYTES)


def _resident(shape):
    zeros = (0,) * len(shape)
    return pl.BlockSpec(shape, lambda *_: zeros, pipeline_mode=pl.Buffered(1))


def _row_tile(n, want):
    t = min(n, want)
    assert n % t == 0, (n, t)
    return t


def _rms(x, g):
    y = x * lax.rsqrt(jnp.mean(x * x, axis=-1, keepdims=True) + EPS)
    return y * g


def _ffn_body(*refs, d_ff, tf, final):
    if final:
        x_ref, g_ref, win_ref, wout_ref, fg_ref, o_ref = refs
    else:
        x_ref, g_ref, win_ref, wout_ref, o_ref = refs
    x = x_ref[...]
    h = _rms(x, g_ref[...]).astype(BF16)
    y = None
    for c in range(d_ff // tf):
        a = jnp.dot(h, win_ref[:, c * tf:(c + 1) * tf], preferred_element_type=F32)
        b = jnp.dot(h, win_ref[:, d_ff + c * tf:d_ff + (c + 1) * tf], preferred_element_type=F32)
        gated = (a * jax.nn.sigmoid(a) * b).astype(BF16)
        part = jnp.dot(gated, wout_ref[c * tf:(c + 1) * tf, :], preferred_element_type=F32)
        y = part if y is None else y + part
    out = x + 0.5 * y
    if final:
        out = _rms(out, fg_ref[...])
    o_ref[...] = out


def _ffn(x, g, w_in, w_out, final_g=None, *, tm=512, tf=1408):
    n, d = x.shape
    d_ff = w_out.shape[0]
    tm = _row_tile(n, tm)
    assert d_ff % tf == 0 and tf % LANES == 0
    final = final_g is not None
    ins = [x, g.reshape(1, d), w_in, w_out]
    specs = [pl.BlockSpec((tm, d), lambda i: (i, 0)), _resident((1, d)),
             _resident(w_in.shape), _resident(w_out.shape)]
    if final:
        ins.append(final_g.reshape(1, d))
        specs.append(_resident((1, d)))
    return pl.pallas_call(
        functools.partial(_ffn_body, d_ff=d_ff, tf=tf, final=final),
        grid=(n // tm,),
        in_specs=specs,
        out_specs=pl.BlockSpec((tm, d), lambda i: (i, 0)),
        out_shape=jax.ShapeDtypeStruct((n, d), F32),
        compiler_params=_params("parallel"),
        name="ffn",
    )(*ins)


def _norm_proj_body(x_ref, g_ref, w_ref, o_ref):
    h = _rms(x_ref[...], g_ref[...]).astype(BF16)
    o_ref[...] = jnp.dot(h, w_ref[...], preferred_element_type=F32)


def _norm_proj(x, g, w, *, tm=256):
    n, d = x.shape
    nout = w.shape[1]
    tm = _row_tile(n, tm)
    return pl.pallas_call(
        _norm_proj_body,
        grid=(n // tm,),
        in_specs=[pl.BlockSpec((tm, d), lambda i: (i, 0)), _resident((1, d)), _resident(w.shape)],
        out_specs=pl.BlockSpec((tm, nout), lambda i: (i, 0)),
        out_shape=jax.ShapeDtypeStruct((n, nout), F32),
        compiler_params=_params("parallel"),
        name="norm_proj",
    )(x, g.reshape(1, d), w)


def _proj_res_body(x_ref, w_ref, r_ref, o_ref):
    o_ref[...] = r_ref[...] + jnp.dot(x_ref[...], w_ref[...], preferred_element_type=F32)


def _proj_res(x, w, res, *, tm=512):
    n, k = x.shape
    d = w.shape[1]
    tm = _row_tile(n, tm)
    return pl.pallas_call(
        _proj_res_body,
        grid=(n // tm,),
        in_specs=[pl.BlockSpec((tm, k), lambda i: (i, 0)), _resident(w.shape),
                  pl.BlockSpec((tm, d), lambda i: (i, 0))],
        out_specs=pl.BlockSpec((tm, d), lambda i: (i, 0)),
        out_shape=jax.ShapeDtypeStruct((n, d), F32),
        compiler_params=_params("parallel"),
        name="proj_res",
    )(x, w, res)


def _ret_body(*refs, nh, dk, dv, c, cp, has_s0):
    if has_s0:
        (q_ref, k_ref, v_ref, g_ref, cos_ref, sin_ref, din_ref, dq_ref, dkk_ref, dc_ref,
         gain_ref, s0_ref, o_ref, s_ref) = refs
    else:
        (q_ref, k_ref, v_ref, g_ref, cos_ref, sin_ref, din_ref, dq_ref, dkk_ref, dc_ref,
         gain_ref, o_ref, s_ref) = refs
    i = pl.program_id(1)

    @pl.when(i == 0)
    def _():
        if has_s0:
            s_ref[...] = s0_ref[...]
        else:
            s_ref[...] = jnp.zeros(s_ref.shape, F32)

    def pad(t):
        if cp == c:
            return t
        return jnp.concatenate([t, jnp.zeros((cp - c, t.shape[1]), t.dtype)], axis=0)

    cos = pad(cos_ref[...])
    sin = pad(sin_ref[...])
    half = dk // 2

    def rope(t):
        t1, t2 = t[:, :half], t[:, half:]
        return jnp.concatenate([t1 * cos - t2 * sin, t1 * sin + t2 * cos], axis=-1)

    for h in range(nh):
        qr = rope(pad(q_ref[0, :, h * dk:(h + 1) * dk]))
        kr = rope(pad(k_ref[0, :, h * dk:(h + 1) * dk])) * (dk ** -0.5)
        qb = qr.astype(BF16)
        kb = kr.astype(BF16)
        vb = pad(v_ref[0, :, h * dv:(h + 1) * dv]).astype(BF16)
        att = lax.dot_general(qb, kb, (((1,), (1,)), ((), ())), preferred_element_type=F32) * din_ref[h]
        o_in = jnp.dot(att.astype(BF16), vb, preferred_element_type=F32)
        s = s_ref[0, h]
        o_x = jnp.dot(qb, s.astype(BF16), preferred_element_type=F32) * dq_ref[h]
        kd = (kr * dkk_ref[h]).T.astype(BF16)
        s_ref[0, h] = s * dc_ref[h] + jnp.dot(kd, vb, preferred_element_type=F32)
        o = (o_in + o_x)[:c]
        mu = jnp.mean(o, axis=-1, keepdims=True)
        d = o - mu
        var = jnp.mean(d * d, axis=-1, keepdims=True)
        on = d * lax.rsqrt(var + EPS) * gain_ref[:, h * dv:(h + 1) * dv]
        gt = g_ref[0, :, h * dv:(h + 1) * dv]
        o_ref[0, :, h * dv:(h + 1) * dv] = (gt * jax.nn.sigmoid(gt) * on).astype(BF16)


def _retention(proj, pos, gn_gain, s0):
    b, l, width = proj.shape
    nh = RET_HEADS
    hk = width // 6
    hv = 2 * hk
    dk, dv = hk // nh, hv // nh
    c = math.gcd(l, RET_CHUNK)
    cp = RET_CHUNK
    n = l // c
    half = dk // 2
    inv = ROPE_BASE ** (-jnp.arange(half, dtype=F32) / half)
    ang = pos.astype(F32)[:, None] * inv[None, :]
    cos, sin = jnp.cos(ang), jnp.sin(ang)
    log_g = jnp.log1p(-jnp.exp2(-5.0 - jnp.arange(nh, dtype=F32)))
    idx = jnp.arange(c, dtype=F32)
    rel = idx[:, None] - idx[None, :]
    d_in = jnp.where(rel[None] >= 0, jnp.exp(log_g[:, None, None] * jnp.maximum(rel, 0.0)[None]), 0.0)
    d_q = jnp.exp(log_g[:, None] * (idx[None, :] + 1.0))
    d_k = jnp.exp(log_g[:, None] * (c - 1.0 - idx[None, :]))
    d_c = jnp.exp(log_g * c)
    d_in = jnp.pad(d_in, ((0, 0), (0, cp - c), (0, cp - c)))
    d_q = jnp.pad(d_q, ((0, 0), (0, cp - c)))[:, :, None]
    d_k = jnp.pad(d_k, ((0, 0), (0, cp - c)))[:, :, None]
    d_c = d_c[:, None, None]

    has_s0 = s0 is not None
    ins = [proj, proj, proj, proj, cos, sin, d_in, d_q, d_k, d_c, gn_gain.reshape(1, hv)]
    specs = [
        pl.BlockSpec((1, c, hk), lambda bi, i: (bi, i, 0)),
        pl.BlockSpec((1, c, hk), lambda bi, i: (bi, i, 1)),
        pl.BlockSpec((1, c, hv), lambda bi, i: (bi, i, 1)),
        pl.BlockSpec((1, c, hv), lambda bi, i: (bi, i, 2)),
        pl.BlockSpec((c, half), lambda bi, i: (i, 0)),
        pl.BlockSpec((c, half), lambda bi, i: (i, 0)),
        _resident(d_in.shape), _resident(d_q.shape), _resident(d_k.shape), _resident(d_c.shape),
        _resident((1, hv)),
    ]
    if has_s0:
        ins.append(s0)
        specs.append(pl.BlockSpec((1, nh, dk, dv), lambda bi, i: (bi, 0, 0, 0)))
    return pl.pallas_call(
        functools.partial(_ret_body, nh=nh, dk=dk, dv=dv, c=c, cp=cp, has_s0=has_s0),
        grid=(b, n),
        in_specs=specs,
        out_specs=[pl.BlockSpec((1, c, hv), lambda bi, i: (bi, i, 0)),
                   pl.BlockSpec((1, nh, dk, dv), lambda bi, i: (bi, 0, 0, 0))],
        out_shape=[jax.ShapeDtypeStruct((b, l, hv), BF16),
                   jax.ShapeDtypeStruct((b, nh, dk, dv), F32)],
        compiler_params=_params("parallel", "arbitrary"),
        name="retention",
    )(*ins)


def _qkv_body(x_ref, g_ref, w_ref, q_ref, k_ref, v_ref, kb_ref, vt_ref, km_ref, *, d, blk, scale):
    h = _rms(x_ref[...], g_ref[...]).astype(BF16)
    y = jnp.dot(h, w_ref[...], preferred_element_type=F32)
    k = y[:, d:2 * d]
    v = y[:, 2 * d:]
    q_ref[...] = (y[:, :d] * scale).astype(BF16)
    k_ref[...] = k
    v_ref[...] = v
    kb_ref[...] = k.astype(BF16)
    vt_ref[0] = v.T.astype(BF16)
    for r in range(k.shape[0] // blk):
        km_ref[r] = jnp.sum(k[r * blk:(r + 1) * blk], axis=0, keepdims=True) * (1.0 / blk)


def _moba_qkv(x, g, w, seq, *, tm=256):
    n, d = x.shape
    hd = d // MOBA_HEADS
    blk = MOBA_BLOCK
    tm = _row_tile(n, tm)
    assert tm % blk == 0 and seq % tm == 0
    per_seq = seq // tm
    row = lambda i: (i, 0)
    return pl.pallas_call(
        functools.partial(_qkv_body, d=d, blk=blk, scale=hd ** -0.5),
        grid=(n // tm,),
        in_specs=[pl.BlockSpec((tm, d), row), _resident((1, d)), _resident(w.shape)],
        out_specs=[pl.BlockSpec((tm, d), row), pl.BlockSpec((tm, d), row), pl.BlockSpec((tm, d), row),
                   pl.BlockSpec((tm, d), row),
                   pl.BlockSpec((1, d, tm), lambda i: (i // per_seq, 0, i % per_seq)),
                   pl.BlockSpec((tm // blk, 1, d), lambda i: (i, 0, 0))],
        out_shape=[jax.ShapeDtypeStruct((n, d), BF16), jax.ShapeDtypeStruct((n, d), F32),
                   jax.ShapeDtypeStruct((n, d), F32), jax.ShapeDtypeStruct((n, d), BF16),
                   jax.ShapeDtypeStruct((n // seq, d, seq), BF16),
                   jax.ShapeDtypeStruct((n // blk, 1, d), F32)],
        compiler_params=_params("parallel"),
        name="moba_qkv",
    )(x, g.reshape(1, d), w)


def _qkv_small_body(x_ref, g_ref, w_ref, q_ref, k_ref, v_ref, *, d, scale):
    h = _rms(x_ref[...], g_ref[...]).astype(BF16)
    y = jnp.dot(h, w_ref[...], preferred_element_type=F32)
    q_ref[...] = y[:, :d] * scale
    k_ref[...] = y[:, d:2 * d]
    v_ref[...] = y[:, 2 * d:]


def _moba_qkv_small(x, g, w):
    n, d = x.shape
    hd = d // MOBA_HEADS
    full = pl.BlockSpec((n, d), lambda i: (0, 0))
    return pl.pallas_call(
        functools.partial(_qkv_small_body, d=d, scale=hd ** -0.5),
        grid=(1,),
        in_specs=[full, _resident((1, d)), _resident(w.shape)],
        out_specs=[full, full, full],
        out_shape=[jax.ShapeDtypeStruct((n, d), F32)] * 3,
        compiler_params=_params("arbitrary"),
        name="moba_qkv_small",
    )(x, g.reshape(1, d), w)


def _t5_bucket_np(dist):
    n = np.maximum(dist, 0)
    max_exact = REL_BUCKETS // 2
    nf = np.maximum(n, 1).astype(np.float64)
    large = max_exact + (np.log(nf / max_exact) / math.log(REL_MAX_DIST / max_exact)
                         * (REL_BUCKETS - max_exact)).astype(np.int64)
    large = np.minimum(large, REL_BUCKETS - 1)
    return np.where(n < max_exact, n, large).astype(np.int32)


def _bias_body(bk_ref, tab_ref, o_ref):
    bk = bk_ref[0]
    acc = jnp.where(bk < 0, NEG_INF, 0.0).astype(F32)
    for b in range(REL_BUCKETS):
        acc = jnp.where(bk == b, tab_ref[0, b], acc)
    o_ref[0, 0] = acc


def _bias_tiles(buckets, table):
    t, r, w = buckets.shape
    g, nbuck, tr, tw = table.shape
    assert nbuck == REL_BUCKETS and tr in (1, r) and tw in (1, w)
    return pl.pallas_call(
        _bias_body,
        grid=(g, t),
        in_specs=[pl.BlockSpec((1, r, w), lambda gi, ti: (ti, 0, 0)),
                  pl.BlockSpec((1, REL_BUCKETS, tr, tw), lambda gi, ti: (gi, 0, 0, 0))],
        out_specs=pl.BlockSpec((1, 1, r, w), lambda gi, ti: (gi, ti, 0, 0)),
        out_shape=jax.ShapeDtypeStruct((g, t, r, w), F32),
        compiler_params=_params("parallel", "parallel"),
        name="t5_bias_tiles",
    )(jnp.asarray(buckets), table)


FLASH_HEADS = 4


def _flash_body(q_ref, k_ref, vt_ref, km_ref, bias_ref, o_ref, *, nb, blk, hd, topk, nch):
    pair = 2 * hd
    lane = lax.broadcasted_iota(jnp.int32, (1, pair), 1)
    row = lax.broadcasted_iota(jnp.int32, (pair, 1), 0)
    bidx = lax.broadcasted_iota(jnp.int32, (nb, 1), 0)
    nt = (((1,), (1,)), ((), ()))

    def lanes(c):
        return slice((c // 2) * pair, (c // 2 + 1) * pair)

    def masked_q(c):
        return jnp.where((lane >= hd) == bool(c % 2), q_ref[0, :, lanes(c)], jnp.zeros((blk, pair), BF16))

    def logits(c, t):
        return lax.dot_general(k_ref[0, 0:(t + 1) * blk, lanes(c)], masked_q(c), nt,
                               preferred_element_type=F32)

    def head(c, t, s):
        qm = masked_q(c)
        parts = [s[t * blk:] + bias_ref[c, 0]]
        if t >= 1:
            gate = lax.dot_general(km_ref[0, :, lanes(c)].astype(BF16), qm, nt, preferred_element_type=F32)
            valid = bidx < t
            gate = jnp.where(valid, gate, NEG_INF)
            rank = jnp.zeros((nb, blk), jnp.int32)
            for j2 in range(t):
                r2 = gate[j2:j2 + 1, :]
                ahead = (r2 > gate) | ((r2 == gate) & (j2 < bidx))
                rank = rank + ahead.astype(jnp.int32)
            sel = valid & (rank < topk)
            parts.append(s[(t - 1) * blk:t * blk] + bias_ref[c, 1]
                         + jnp.where(sel[t - 1:t], 0.0, NEG_INF))
        if t >= 2:
            far = jnp.where(sel[:t - 1], bias_ref[c, 2, 0:1, :], NEG_INF)
            parts.append((s[:(t - 1) * blk].reshape(t - 1, blk, blk) + far[:, None, :]
                          ).reshape((t - 1) * blk, blk))
        m = parts[0].max(axis=0, keepdims=True)
        for x in parts[1:]:
            m = jnp.maximum(m, x.max(axis=0, keepdims=True))
        ps = [jnp.exp(x - m) for x in parts]
        l = ps[0].sum(axis=0, keepdims=True)
        for x in ps[1:]:
            l = l + x.sum(axis=0, keepdims=True)
        p = jnp.concatenate([x.astype(BF16) for x in reversed(ps)], axis=0)
        acc = jnp.dot(vt_ref[0, lanes(c), 0:(t + 1) * blk], p, preferred_element_type=F32)
        return acc / l

    def q_block(t):
        s_next = logits(0, t)
        res = []
        for c in range(nch):
            s = s_next
            if c + 1 < nch:
                s_next = logits(c + 1, t)
            res.append(head(c, t, s))
        for pc in range(nch // 2):
            out_t = jnp.where(row >= hd, res[2 * pc + 1], res[2 * pc])
            o_ref[0, :, pc * pair:(pc + 1) * pair] = out_t.T.astype(BF16)

    for t in range(nb):
        pl.when(pl.program_id(2) == t)(functools.partial(q_block, t))


def _moba_flash(q, kb, vt, kmean, bias):
    b, l, d = q.shape
    hd = d // MOBA_HEADS
    pair = 2 * hd
    assert pair == LANES
    blk = MOBA_BLOCK
    nb = l // blk
    nch = FLASH_HEADS
    w = nch * hd
    assert nch % 2 == 0 and d % w == 0
    return pl.pallas_call(
        functools.partial(_flash_body, nb=nb, blk=blk, hd=hd, topk=MOBA_TOPK, nch=nch),
        grid=(b, d // w, nb),
        in_specs=[pl.BlockSpec((1, blk, w), lambda bi, p, qi: (bi, qi, p)),
                  pl.BlockSpec((1, l, w), lambda bi, p, qi: (bi, 0, p)),
                  pl.BlockSpec((1, w, l), lambda bi, p, qi: (bi, p, 0)),
                  pl.BlockSpec((1, nb, w), lambda bi, p, qi: (bi, 0, p)),
                  pl.BlockSpec((nch, 3, blk, blk), lambda bi, p, qi: (p, 0, 0, 0))],
        out_specs=pl.BlockSpec((1, blk, w), lambda bi, p, qi: (bi, qi, p)),
        out_shape=jax.ShapeDtypeStruct((b, l, d), BF16),
        compiler_params=_params("parallel", "parallel", "arbitrary"),
        name="moba_flash",
    )(q, kb, vt, kmean, bias)


def _prompt_bias_buckets(blk):
    assert REL_MAX_DIST <= blk + 1
    key = np.arange(blk)[:, None]
    qry = np.arange(blk)[None, :]
    tiles = []
    for dblk in range(3):
        dist = dblk * blk + qry - key
        tiles.append(np.where(dist >= 0, _t5_bucket_np(dist), -1))
    return np.stack(tiles).astype(np.int32)


SAMPLE_BLOCKS_PER_STEP = 4


def _sample_body(pt_ref, bsel_ref, qbdt_ref, *refs, nbp, bps, heads, lq, topk):
    npg = 2 * bps
    k_refs, v_refs = refs[:npg], refs[npg:2 * npg]
    knt_ref, vnt_ref, bias_ref, bias_own_ref, o_ref, m_ref, l_ref, g_ref, acc_ref = refs[2 * npg:]
    step = pl.program_id(1)
    hq = heads * lq
    qbdt = qbdt_ref[0]
    d = qbdt.shape[1]
    hd = d // heads
    lane = lax.broadcasted_iota(jnp.int32, (1, LANES), 1)
    nt = (((1,), (1,)), ((), ()))

    @pl.when(step == 0)
    def _():
        m_ref[...] = jnp.zeros(m_ref.shape, F32)
        l_ref[...] = jnp.zeros(l_ref.shape, F32)
        g_ref[...] = jnp.zeros(g_ref.shape, F32)

    def block(kt, vt, bias):
        sr = jnp.dot(qbdt, kt.astype(BF16), preferred_element_type=F32)
        g = jnp.sum(sr, axis=1, keepdims=True) * (1.0 / kt.shape[1])
        sc = sr + bias
        m = jnp.max(sc, axis=1, keepdims=True)
        p = jnp.exp(sc - m)
        l = jnp.sum(p, axis=1, keepdims=True)
        acc = lax.dot_general(p.astype(BF16), vt.astype(BF16), nt, preferred_element_type=F32)
        slab = jnp.concatenate(
            [acc[h * lq:(h + 1) * lq, (h * hd // LANES) * LANES:(h * hd // LANES + 1) * LANES]
             for h in range(heads)], axis=0)
        return m, l, g, slab

    stats = []
    for jj in range(bps):
        j = step * bps + jj
        kt = jnp.concatenate([k_refs[2 * jj][0], k_refs[2 * jj + 1][0]], axis=1)
        vt = jnp.concatenate([v_refs[2 * jj][0], v_refs[2 * jj + 1][0]], axis=1)
        m, l, g, slab = block(kt, vt, bias_ref[bsel_ref[j]])
        acc_ref[j] = slab
        stats.append((j, m, l, g))
    m_all, l_all, g_all = m_ref[...], l_ref[...], g_ref[...]
    for j, m, l, g in stats:
        hit = lane == j
        m_all = jnp.where(hit, m, m_all)
        l_all = jnp.where(hit, l, l_all)
        g_all = jnp.where(hit, g, g_all)
    m_ref[...] = m_all
    l_ref[...] = l_all
    g_ref[...] = g_all

    @pl.when(step == nbp // bps - 1)
    def _():
        m, l, _, slab = block(knt_ref[0], vnt_ref[0], bias_own_ref[...])
        acc_ref[nbp] = slab
        own = lane == nbp
        m_fin = jnp.where(own, m, m_all)
        l_fin = jnp.where(own, l, l_all)
        lane_f = lane.astype(F32)
        gate = jnp.where(lane < nbp, g_all, NEG_INF)
        sel = jnp.broadcast_to(own, (hq, LANES))
        for _ in range(min(topk, nbp)):
            best = jnp.max(gate, axis=1, keepdims=True)
            first = jnp.min(jnp.where(gate == best, lane_f, float(LANES)), axis=1, keepdims=True)
            pick = lane_f == first
            sel = sel | pick
            gate = jnp.where(pick, NEG_INF, gate)
        top = jnp.max(jnp.where(sel, m_fin, NEG_INF), axis=1, keepdims=True)
        w = jnp.where(sel, jnp.exp(m_fin - top), 0.0)
        wn = w / jnp.sum(w * l_fin, axis=1, keepdims=True)
        out = jnp.zeros((hq, LANES), F32)
        for jj in range(nbp + 1):
            out = out + wn[:, jj:jj + 1] * acc_ref[jj]
        rrow = lax.broadcasted_iota(jnp.int32, (hq, 1), 0)
        keep = (lane // hd) == ((rrow // lq) % (LANES // hd))
        o_ref[0] = jnp.where(keep, out, 0.0)


def _moba_sample(q, k_new, v_new, cache_kt, cache_vt, page_base, page_table, rel_bias):
    b, lq, d = q.shape
    heads = MOBA_HEADS
    hd = d // heads
    hq = heads * lq
    blk = MOBA_BLOCK
    n_pages = page_table.shape[1]
    past = n_pages * PAGE_SIZE
    assert hq == LANES and blk == 2 * PAGE_SIZE and past % blk == 0 and lq <= PAGE_SIZE
    nbp = past // blk
    bps = math.gcd(nbp, SAMPLE_BLOCKS_PER_STEP)
    assert nbp + 1 <= LANES

    qh = q.reshape(b, lq, heads, hd).transpose(0, 2, 1, 3)
    eye = jnp.eye(heads, dtype=F32)
    qbdt = (qh[:, :, :, None, :] * eye[None, :, None, :, None]).reshape(b, hq, d).astype(BF16)

    qry = np.tile(np.arange(lq), heads)[:, None]
    key = np.arange(blk)[None, :]
    tiles, bsel = [], []
    for jb in range(nbp):
        dist = past + qry - (jb * blk + key)
        bk = np.where(dist >= 0, _t5_bucket_np(dist), -1).astype(np.int32)
        for t, existing in enumerate(tiles):
            if np.array_equal(existing, bk):
                bsel.append(t)
                break
        else:
            tiles.append(bk)
            bsel.append(len(tiles) - 1)
    table = jnp.repeat(rel_bias, lq, axis=1)[None, :, :, None]
    bias = _bias_tiles(np.stack(tiles), table)[0]
    dist_own = qry - np.arange(PAGE_SIZE)[None, :]
    own_bk = np.where(dist_own >= 0, _t5_bucket_np(dist_own), -1).astype(np.int32)
    bias_own = _bias_tiles(own_bk[None], table)[0, 0]
    nt = bias.shape[0]

    knt = jnp.pad(k_new.transpose(0, 2, 1), ((0, 0), (0, 0), (0, PAGE_SIZE - lq)))
    vnt = jnp.pad(v_new.transpose(0, 2, 1), ((0, 0), (0, 0), (0, PAGE_SIZE - lq)))
    npg = 2 * bps

    def page(t):
        return lambda bi, s, pt, bs: (page_base + pt[bi * n_pages + s * npg + t], 0, 0)

    per_b = lambda bi, s, pt, bs: (bi, 0, 0)
    page_specs = [pl.BlockSpec((1, d, PAGE_SIZE), page(t)) for t in range(npg)]
    grid_spec = pltpu.PrefetchScalarGridSpec(
        num_scalar_prefetch=2,
        grid=(b, nbp // bps),
        in_specs=[pl.BlockSpec((1, hq, d), per_b)] + page_specs + page_specs + [
            pl.BlockSpec((1, d, PAGE_SIZE), per_b), pl.BlockSpec((1, d, PAGE_SIZE), per_b),
            pl.BlockSpec((nt, hq, blk), lambda bi, s, pt, bs: (0, 0, 0)),
            pl.BlockSpec((hq, PAGE_SIZE), lambda bi, s, pt, bs: (0, 0))],
        out_specs=pl.BlockSpec((1, hq, LANES), per_b),
        scratch_shapes=[pltpu.VMEM((hq, LANES), F32), pltpu.VMEM((hq, LANES), F32),
                        pltpu.VMEM((hq, LANES), F32), pltpu.VMEM((nbp + 1, hq, LANES), F32)],
    )
    o2 = pl.pallas_call(
        functools.partial(_sample_body, nbp=nbp, bps=bps, heads=heads, lq=lq, topk=MOBA_TOPK),
        grid_spec=grid_spec,
        out_shape=jax.ShapeDtypeStruct((b, hq, LANES), F32),
        compiler_params=_params("parallel", "arbitrary"),
        name="moba_sample",
    )(page_table.reshape(-1), jnp.asarray(np.array(bsel, np.int32)), qbdt,
      *([cache_kt] * npg), *([cache_vt] * npg), knt, vnt, bias, bias_own)
    o = o2.reshape(b, heads, lq, LANES // hd, hd).sum(axis=3)
    return o.transpose(0, 2, 1, 3).reshape(b, lq, d).astype(BF16)


def kernel(x_prompt, x_sample, state_ret, cache_k, cache_v, page_table, ffn1_norm, ffn1_w_in, ffn1_w_out,
           mix_norm, ffn2_norm, ffn2_w_in, ffn2_w_out, ret_w_in, ret_gn_gain, ret_w_out, moba_w_qkv,
           moba_w_out, rel_bias, final_norm):
    bp, lp, d = x_prompt.shape
    bs, ls, _ = x_sample.shape
    depth = ffn1_norm.shape[0]
    past = page_table.shape[1] * PAGE_SIZE
    pos_p = jnp.arange(lp, dtype=jnp.int32)
    pos_s = past + jnp.arange(ls, dtype=jnp.int32)
    hp = x_prompt.reshape(bp * lp, d)
    hs = x_sample.reshape(bs * ls, d)
    cast = lambda w: w.astype(BF16)
    ret_p, ret_s, kp_l, vp_l, ks_l, vs_l = [], [], [], [], [], []
    for i in range(depth):
        w1i, w1o = cast(ffn1_w_in[i]), cast(ffn1_w_out[i])
        hp = _ffn(hp, ffn1_norm[i], w1i, w1o)
        hs = _ffn(hs, ffn1_norm[i], w1i, w1o)
        if i % 2 == 0:
            r = i // 2
            w_in, w_out = cast(ret_w_in[r]), cast(ret_w_out[r])
            proj_p = _norm_proj(hp, mix_norm[i], w_in).reshape(bp, lp, -1)
            proj_s = _norm_proj(hs, mix_norm[i], w_in).reshape(bs, ls, -1)
            gp, sp = _retention(proj_p, pos_p, ret_gn_gain[r], None)
            gs, ss = _retention(proj_s, pos_s, ret_gn_gain[r], state_ret[r])
            hp = _proj_res(gp.reshape(bp * lp, -1), w_out, hp)
            hs = _proj_res(gs.reshape(bs * ls, -1), w_out, hs)
            ret_p.append(sp)
            ret_s.append(ss)
        else:
            m = i // 2
            w_qkv, w_out = cast(moba_w_qkv[m]), cast(moba_w_out[m])
            hd = d // MOBA_HEADS
            q, k, v, kb, vt, km = _moba_qkv(hp, mix_norm[i], w_qkv, lp)
            bias = _bias_tiles(_prompt_bias_buckets(MOBA_BLOCK), rel_bias.T[:, :, None, None])
            op = _moba_flash(q.reshape(bp, lp, d), kb.reshape(bp, lp, d), vt,
                             km.reshape(bp, lp // MOBA_BLOCK, d), bias)
            hp = _proj_res(op.reshape(bp * lp, d), w_out, hp)
            kp_l.append(k.reshape(bp, lp, MOBA_HEADS, hd))
            vp_l.append(v.reshape(bp, lp, MOBA_HEADS, hd))

            qs, ksn, vsn = _moba_qkv_small(hs, mix_norm[i], w_qkv)
            n_pool = cache_k.shape[1]
            pages_t = lambda cache: cache.transpose(0, 1, 3, 4, 2).reshape(-1, d, PAGE_SIZE)
            os_ = _moba_sample(qs.reshape(bs, ls, d), ksn.reshape(bs, ls, d), vsn.reshape(bs, ls, d),
                               pages_t(cache_k), pages_t(cache_v), m * n_pool, page_table, rel_bias)
            hs = _proj_res(os_.reshape(bs * ls, d), w_out, hs)
            ks_l.append(ksn.reshape(bs, ls, MOBA_HEADS, hd))
            vs_l.append(vsn.reshape(bs, ls, MOBA_HEADS, hd))
        last = i == depth - 1
        w2i, w2o = cast(ffn2_w_in[i]), cast(ffn2_w_out[i])
        hp = _ffn(hp, ffn2_norm[i], w2i, w2o, final_norm if last else None)
        hs = _ffn(hs, ffn2_norm[i], w2i, w2o, final_norm if last else None)
    return (hp.reshape(bp, lp, d), hs.reshape(bs, ls, d), jnp.stack(ret_p), jnp.stack(ret_s),
            jnp.stack(kp_l), jnp.stack(vp_l), jnp.stack(ks_l), jnp.stack(vs_l))
```

```python
import functools
import math

import jax
import jax.numpy as jnp
import numpy as np
from jax import lax
from jax.experimental import pallas as pl
from jax.experimental.pallas import tpu as pltpu

F32 = jnp.float32
BF16 = jnp.bfloat16
NEG_INF = float("-inf")
LOG2E = math.log2(math.e)

LANES = 128
SUBLANES = 8
VMEM_LIMIT_BYTES = 56 * 1024 * 1024

EPS = 1e-6
ROPE_BASE = 10000.0
RET_HEADS = 4
RET_CHUNK = 128
MOBA_HEADS = 16
MOBA_BLOCK = 256
MOBA_TOPK = 3
PAGE_SIZE = 128
REL_BUCKETS = 32
REL_MAX_DIST = 128


def _params(*semantics):
    return pltpu.CompilerParams(dimension_semantics=semantics, vmem_limit_bytes=VMEM_LIMIT_BYTES)


def _resident(shape):
    zeros = (0,) * len(shape)
    return pl.BlockSpec(shape, lambda *_: zeros, pipeline_mode=pl.Buffered(1))


def _row_tile(n, want):
    t = min(n, want)
    assert n % t == 0, (n, t)
    return t


def _rms(x, g):
    y = x * lax.rsqrt(jnp.mean(x * x, axis=-1, keepdims=True) + EPS)
    return y * g


def _ffn_body(*refs, d_ff, tf, final):
    if final:
        x_ref, g_ref, win_ref, wout_ref, fg_ref, o_ref = refs
    else:
        x_ref, g_ref, win_ref, wout_ref, o_ref = refs
    x = x_ref[...]
    h = _rms(x, g_ref[...]).astype(BF16)
    y = None
    for c in range(d_ff // tf):
        a = jnp.dot(h, win_ref[:, c * tf:(c + 1) * tf], preferred_element_type=F32)
        b = jnp.dot(h, win_ref[:, d_ff + c * tf:d_ff + (c + 1) * tf], preferred_element_type=F32)
        gated = (a * jax.nn.sigmoid(a) * b).astype(BF16)
        part = jnp.dot(gated, wout_ref[c * tf:(c + 1) * tf, :], preferred_element_type=F32)
        y = part if y is None else y + part
    out = x + 0.5 * y
    if final:
        out = _rms(out, fg_ref[...])
    o_ref[...] = out


MXU_TILE = 256


def _ffn(x, g, w_in, w_out, final_g=None, *, tm=512):
    n, d = x.shape
    d_ff = w_out.shape[0]
    tm = _row_tile(n, tm)
    tf = d_ff if d_ff % MXU_TILE == 0 else LANES
    assert d_ff % tf == 0
    final = final_g is not None
    ins = [x, g.reshape(1, d), w_in, w_out]
    specs = [pl.BlockSpec((tm, d), lambda i: (i, 0)), _resident((1, d)),
             _resident(w_in.shape), _resident(w_out.shape)]
    if final:
        ins.append(final_g.reshape(1, d))
        specs.append(_resident((1, d)))
    return pl.pallas_call(
        functools.partial(_ffn_body, d_ff=d_ff, tf=tf, final=final),
        grid=(n // tm,),
        in_specs=specs,
        out_specs=pl.BlockSpec((tm, d), lambda i: (i, 0)),
        out_shape=jax.ShapeDtypeStruct((n, d), F32),
        compiler_params=_params("parallel"),
        name="ffn",
    )(*ins)


def _norm_proj_body(x_ref, g_ref, w_ref, o_ref):
    h = _rms(x_ref[...], g_ref[...]).astype(BF16)
    o_ref[...] = jnp.dot(h, w_ref[...], preferred_element_type=F32)


def _norm_proj(x, g, w, *, tm=256):
    n, d = x.shape
    nout = w.shape[1]
    tm = _row_tile(n, tm)
    return pl.pallas_call(
        _norm_proj_body,
        grid=(n // tm,),
        in_specs=[pl.BlockSpec((tm, d), lambda i: (i, 0)), _resident((1, d)), _resident(w.shape)],
        out_specs=pl.BlockSpec((tm, nout), lambda i: (i, 0)),
        out_shape=jax.ShapeDtypeStruct((n, nout), F32),
        compiler_params=_params("parallel"),
        name="norm_proj",
    )(x, g.reshape(1, d), w)


def _proj_res_body(x_ref, w_ref, r_ref, o_ref):
    o_ref[...] = r_ref[...] + jnp.dot(x_ref[...], w_ref[...], preferred_element_type=F32)


def _proj_res(x, w, res, *, tm=512):
    n, k = x.shape
    d = w.shape[1]
    tm = _row_tile(n, tm)
    return pl.pallas_call(
        _proj_res_body,
        grid=(n // tm,),
        in_specs=[pl.BlockSpec((tm, k), lambda i: (i, 0)), _resident(w.shape),
                  pl.BlockSpec((tm, d), lambda i: (i, 0))],
        out_specs=pl.BlockSpec((tm, d), lambda i: (i, 0)),
        out_shape=jax.ShapeDtypeStruct((n, d), F32),
        compiler_params=_params("parallel"),
        name="proj_res",
    )(x, w, res)


def _ret_body(*refs, nh, dk, dv, c, cp, has_s0):
    if has_s0:
        (q_ref, k_ref, v_ref, g_ref, cos_ref, sin_ref, din_ref, dq_ref, dkk_ref, dc_ref,
         gain_ref, s0_ref, o_ref, s_ref) = refs
    else:
        (q_ref, k_ref, v_ref, g_ref, cos_ref, sin_ref, din_ref, dq_ref, dkk_ref, dc_ref,
         gain_ref, o_ref, s_ref) = refs
    i = pl.program_id(1)

    @pl.when(i == 0)
    def _():
        if has_s0:
            s_ref[...] = s0_ref[...]
        else:
            s_ref[...] = jnp.zeros(s_ref.shape, F32)

    def pad(t):
        if cp == c:
            return t
        return jnp.concatenate([t, jnp.zeros((cp - c, t.shape[1]), t.dtype)], axis=0)

    cos = pad(cos_ref[...])
    sin = pad(sin_ref[...])
    half = dk // 2

    def rope(t):
        t1, t2 = t[:, :half], t[:, half:]
        return jnp.concatenate([t1 * cos - t2 * sin, t1 * sin + t2 * cos], axis=-1)

    for h in range(nh):
        qr = rope(pad(q_ref[0, :, h * dk:(h + 1) * dk]))
        kr = rope(pad(k_ref[0, :, h * dk:(h + 1) * dk])) * (dk ** -0.5)
        qb = qr.astype(BF16)
        kb = kr.astype(BF16)
        vb = pad(v_ref[0, :, h * dv:(h + 1) * dv]).astype(BF16)
        att = lax.dot_general(qb, kb, (((1,), (1,)), ((), ())), preferred_element_type=F32) * din_ref[h]
        o_in = jnp.dot(att.astype(BF16), vb, preferred_element_type=F32)
        s = s_ref[0, h]
        o_x = jnp.dot(qb, s.astype(BF16), preferred_element_type=F32) * dq_ref[h]
        kd = (kr * dkk_ref[h]).T.astype(BF16)
        s_ref[0, h] = s * dc_ref[h] + jnp.dot(kd, vb, preferred_element_type=F32)
        o = (o_in + o_x)[:c]
        mu = jnp.mean(o, axis=-1, keepdims=True)
        d = o - mu
        var = jnp.mean(d * d, axis=-1, keepdims=True)
        on = d * lax.rsqrt(var + EPS) * gain_ref[:, h * dv:(h + 1) * dv]
        gt = g_ref[0, :, h * dv:(h + 1) * dv]
        o_ref[0, :, h * dv:(h + 1) * dv] = (gt * jax.nn.sigmoid(gt) * on).astype(BF16)


def _retention(proj, pos, gn_gain, s0):
    b, l, width = proj.shape
    nh = RET_HEADS
    hk = width // 6
    hv = 2 * hk
    dk, dv = hk // nh, hv // nh
    c = math.gcd(l, RET_CHUNK)
    cp = RET_CHUNK
    n = l // c
    half = dk // 2
    inv = ROPE_BASE ** (-jnp.arange(half, dtype=F32) / half)
    ang = pos.astype(F32)[:, None] * inv[None, :]
    cos, sin = jnp.cos(ang), jnp.sin(ang)
    log_g = jnp.log1p(-jnp.exp2(-5.0 - jnp.arange(nh, dtype=F32)))
    idx = jnp.arange(c, dtype=F32)
    rel = idx[:, None] - idx[None, :]
    d_in = jnp.where(rel[None] >= 0, jnp.exp(log_g[:, None, None] * jnp.maximum(rel, 0.0)[None]), 0.0)
    d_q = jnp.exp(log_g[:, None] * (idx[None, :] + 1.0))
    d_k = jnp.exp(log_g[:, None] * (c - 1.0 - idx[None, :]))
    d_c = jnp.exp(log_g * c)
    d_in = jnp.pad(d_in, ((0, 0), (0, cp - c), (0, cp - c)))
    d_q = jnp.pad(d_q, ((0, 0), (0, cp - c)))[:, :, None]
    d_k = jnp.pad(d_k, ((0, 0), (0, cp - c)))[:, :, None]
    d_c = d_c[:, None, None]

    has_s0 = s0 is not None
    ins = [proj, proj, proj, proj, cos, sin, d_in, d_q, d_k, d_c, gn_gain.reshape(1, hv)]
    specs = [
        pl.BlockSpec((1, c, hk), lambda bi, i: (bi, i, 0)),
        pl.BlockSpec((1, c, hk), lambda bi, i: (bi, i, 1)),
        pl.BlockSpec((1, c, hv), lambda bi, i: (bi, i, 1)),
        pl.BlockSpec((1, c, hv), lambda bi, i: (bi, i, 2)),
        pl.BlockSpec((c, half), lambda bi, i: (i, 0)),
        pl.BlockSpec((c, half), lambda bi, i: (i, 0)),
        _resident(d_in.shape), _resident(d_q.shape), _resident(d_k.shape), _resident(d_c.shape),
        _resident((1, hv)),
    ]
    if has_s0:
        ins.append(s0)
        specs.append(pl.BlockSpec((1, nh, dk, dv), lambda bi, i: (bi, 0, 0, 0)))
    return pl.pallas_call(
        functools.partial(_ret_body, nh=nh, dk=dk, dv=dv, c=c, cp=cp, has_s0=has_s0),
        grid=(b, n),
        in_specs=specs,
        out_specs=[pl.BlockSpec((1, c, hv), lambda bi, i: (bi, i, 0)),
                   pl.BlockSpec((1, nh, dk, dv), lambda bi, i: (bi, 0, 0, 0))],
        out_shape=[jax.ShapeDtypeStruct((b, l, hv), BF16),
                   jax.ShapeDtypeStruct((b, nh, dk, dv), F32)],
        compiler_params=_params("parallel", "arbitrary"),
        name="retention",
    )(*ins)


def _qkv_body(x_ref, g_ref, w_ref, q_ref, k_ref, v_ref, kb_ref, vt_ref, km_ref, *, d, blk, scale):
    h = _rms(x_ref[...], g_ref[...]).astype(BF16)
    y = jnp.dot(h, w_ref[...], preferred_element_type=F32)
    k = y[:, d:2 * d]
    v = y[:, 2 * d:]
    q_ref[...] = (y[:, :d] * scale).astype(BF16)
    k_ref[...] = k
    v_ref[...] = v
    kb_ref[...] = k.astype(BF16)
    vt_ref[0] = v.T.astype(BF16)
    for r in range(k.shape[0] // blk):
        km_ref[r] = jnp.sum(k[r * blk:(r + 1) * blk], axis=0, keepdims=True) * (1.0 / blk)


def _moba_qkv(x, g, w, seq, *, tm=256):
    n, d = x.shape
    hd = d // MOBA_HEADS
    blk = MOBA_BLOCK
    tm = _row_tile(n, tm)
    assert tm % blk == 0 and seq % tm == 0
    per_seq = seq // tm
    row = lambda i: (i, 0)
    return pl.pallas_call(
        functools.partial(_qkv_body, d=d, blk=blk, scale=LOG2E * hd ** -0.5),
        grid=(n // tm,),
        in_specs=[pl.BlockSpec((tm, d), row), _resident((1, d)), _resident(w.shape)],
        out_specs=[pl.BlockSpec((tm, d), row), pl.BlockSpec((tm, d), row), pl.BlockSpec((tm, d), row),
                   pl.BlockSpec((tm, d), row),
                   pl.BlockSpec((1, d, tm), lambda i: (i // per_seq, 0, i % per_seq)),
                   pl.BlockSpec((tm // blk, 1, d), lambda i: (i, 0, 0))],
        out_shape=[jax.ShapeDtypeStruct((n, d), BF16), jax.ShapeDtypeStruct((n, d), F32),
                   jax.ShapeDtypeStruct((n, d), F32), jax.ShapeDtypeStruct((n, d), BF16),
                   jax.ShapeDtypeStruct((n // seq, d, seq), BF16),
                   jax.ShapeDtypeStruct((n // blk, 1, d), F32)],
        compiler_params=_params("parallel"),
        name="moba_qkv",
    )(x, g.reshape(1, d), w)


def _qkv_small_body(x_ref, g_ref, w_ref, q_ref, k_ref, v_ref, *, d, scale):
    h = _rms(x_ref[...], g_ref[...]).astype(BF16)
    y = jnp.dot(h, w_ref[...], preferred_element_type=F32)
    q_ref[...] = y[:, :d] * scale
    k_ref[...] = y[:, d:2 * d]
    v_ref[...] = y[:, 2 * d:]


def _moba_qkv_small(x, g, w):
    n, d = x.shape
    hd = d // MOBA_HEADS
    full = pl.BlockSpec((n, d), lambda i: (0, 0))
    return pl.pallas_call(
        functools.partial(_qkv_small_body, d=d, scale=hd ** -0.5),
        grid=(1,),
        in_specs=[full, _resident((1, d)), _resident(w.shape)],
        out_specs=[full, full, full],
        out_shape=[jax.ShapeDtypeStruct((n, d), F32)] * 3,
        compiler_params=_params("arbitrary"),
        name="moba_qkv_small",
    )(x, g.reshape(1, d), w)


def _t5_bucket_np(dist):
    n = np.maximum(dist, 0)
    max_exact = REL_BUCKETS // 2
    nf = np.maximum(n, 1).astype(np.float64)
    large = max_exact + (np.log(nf / max_exact) / math.log(REL_MAX_DIST / max_exact)
                         * (REL_BUCKETS - max_exact)).astype(np.int64)
    large = np.minimum(large, REL_BUCKETS - 1)
    return np.where(n < max_exact, n, large).astype(np.int32)


def _bias_body(bk_ref, tab_ref, o_ref, *, scale):
    bk = bk_ref[0]
    acc = jnp.where(bk < 0, NEG_INF, 0.0).astype(F32)
    for b in range(REL_BUCKETS):
        acc = jnp.where(bk == b, tab_ref[0, b] * scale, acc)
    o_ref[0, 0] = acc


def _bias_tiles(buckets, table, scale=1.0):
    t, r, w = buckets.shape
    g, nbuck, tr, tw = table.shape
    assert nbuck == REL_BUCKETS and tr in (1, r) and tw in (1, w)
    return pl.pallas_call(
        functools.partial(_bias_body, scale=scale),
        grid=(g, t),
        in_specs=[pl.BlockSpec((1, r, w), lambda gi, ti: (ti, 0, 0)),
                  pl.BlockSpec((1, REL_BUCKETS, tr, tw), lambda gi, ti: (gi, 0, 0, 0))],
        out_specs=pl.BlockSpec((1, 1, r, w), lambda gi, ti: (gi, ti, 0, 0)),
        out_shape=jax.ShapeDtypeStruct((g, t, r, w), F32),
        compiler_params=_params("parallel", "parallel"),
        name="t5_bias_tiles",
    )(jnp.asarray(buckets), table)


FLASH_HEADS = 4


def _flash_body(q_ref, k_ref, vt_ref, km_ref, bias_ref, o_ref, *, nb, blk, hd, topk, nch):
    pair = 2 * hd
    lane = lax.broadcasted_iota(jnp.int32, (1, pair), 1)
    row = lax.broadcasted_iota(jnp.int32, (pair, 1), 0)
    bidx = lax.broadcasted_iota(jnp.int32, (nb, 1), 0)
    nt = (((1,), (1,)), ((), ()))

    wq = 2 * blk

    def lanes(pc):
        return slice(pc * pair, (pc + 1) * pair)

    def both(pc, idx):
        return jnp.concatenate([bias_ref[(2 * pc,) + idx], bias_ref[(2 * pc + 1,) + idx]], axis=1)

    def masked_q(pc):
        qp = q_ref[0, :, lanes(pc)]
        zero = jnp.zeros((blk, pair), BF16)
        return jnp.concatenate([jnp.where(lane < hd, qp, zero), jnp.where(lane >= hd, qp, zero)], axis=0)

    def q_block(t):
        npair = nch // 2
        qs = [masked_q(pc) for pc in range(npair)]
        adds = []
        for pc in range(npair):
            add = [None] * (t + 1)
            add[t] = both(pc, (0,))
            if t >= 1:
                gate = lax.dot_general(km_ref[0, :, lanes(pc)].astype(BF16), qs[pc], nt,
                                       preferred_element_type=F32)
                valid = bidx < t
                gate = jnp.where(valid, gate, NEG_INF)
                rank = jnp.zeros((nb, wq), jnp.int32)
                for j2 in range(t):
                    r2 = gate[j2:j2 + 1, :]
                    ahead = (r2 > gate) | ((r2 == gate) & (j2 < bidx))
                    rank = rank + ahead.astype(jnp.int32)
                sel = valid & (rank < topk)
                add[t - 1] = both(pc, (1,)) + jnp.where(sel[t - 1:t], 0.0, NEG_INF)
                far = jnp.where(sel, both(pc, (2, slice(0, 1))), NEG_INF)
                for j in range(t - 1):
                    add[j] = far[j:j + 1]
            adds.append(add)

        order = [t] + list(range(t))

        def qk(pc, j):
            return lax.dot_general(k_ref[0, j * blk:(j + 1) * blk, lanes(pc)], qs[pc], nt,
                                   preferred_element_type=F32)

        def pv(pc, j, p):
            return jnp.dot(vt_ref[0, lanes(pc), j * blk:(j + 1) * blk], p, preferred_element_type=F32)

        state = [dict(x={}, m=None, l=None, acc=None) for _ in range(npair)]

        def pass1(pc):
            st = state[pc]
            pend = None
            for j in order:
                s = qk(pc, j)
                yield
                if pend is not None:
                    finish(st, pc, *pend)
                pend = (j, s)
            finish(st, pc, *pend)
            yield

        def finish(st, pc, j, s):
            x = s + adds[pc][j]
            st["x"][j] = x
            mx = x.max(axis=0, keepdims=True)
            st["m"] = mx if st["m"] is None else jnp.maximum(st["m"], mx)

        def pass2(pc):
            st = state[pc]
            pend = None
            for j in order:
                p = jnp.exp2(st["x"][j] - st["m"])
                ps = p.sum(axis=0, keepdims=True)
                st["l"] = ps if st["l"] is None else st["l"] + ps
                yield
                if pend is not None:
                    a = pv(pc, *pend)
                    st["acc"] = a if st["acc"] is None else st["acc"] + a
                pend = (j, p.astype(BF16))
            a = pv(pc, *pend)
            st["acc"] = a if st["acc"] is None else st["acc"] + a
            yield

        def run_together(*gens):
            gens = list(gens)
            while gens:
                for g in list(gens):
                    try:
                        next(g)
                    except StopIteration:
                        gens.remove(g)

        run_together(pass1(0))
        for pc in range(npair):
            if pc + 1 < npair:
                run_together(pass1(pc + 1), pass2(pc))
            else:
                run_together(pass2(pc))
        for pc in range(npair):
            st = state[pc]
            acc = st["acc"] / st["l"]
            out_t = jnp.where(row >= hd, acc[:, blk:], acc[:, :blk])
            o_ref[0, :, lanes(pc)] = out_t.T.astype(BF16)

    for t in range(nb):
        pl.when(pl.program_id(2) == t)(functools.partial(q_block, t))


def _moba_flash(q, kb, vt, kmean, bias):
    b, l, d = q.shape
    hd = d // MOBA_HEADS
    pair = 2 * hd
    assert pair == LANES
    blk = MOBA_BLOCK
    nb = l // blk
    nch = FLASH_HEADS
    w = nch * hd
    assert nch % 2 == 0 and d % w == 0
    return pl.pallas_call(
        functools.partial(_flash_body, nb=nb, blk=blk, hd=hd, topk=MOBA_TOPK, nch=nch),
        grid=(b, d // w, nb),
        in_specs=[pl.BlockSpec((1, blk, w), lambda bi, p, qi: (bi, qi, p)),
                  pl.BlockSpec((1, l, w), lambda bi, p, qi: (bi, 0, p)),
                  pl.BlockSpec((1, w, l), lambda bi, p, qi: (bi, p, 0)),
                  pl.BlockSpec((1, nb, w), lambda bi, p, qi: (bi, 0, p)),
                  pl.BlockSpec((nch, 3, blk, blk), lambda bi, p, qi: (p, 0, 0, 0))],
        out_specs=pl.BlockSpec((1, blk, w), lambda bi, p, qi: (bi, qi, p)),
        out_shape=jax.ShapeDtypeStruct((b, l, d), BF16),
        compiler_params=_params("parallel", "parallel", "arbitrary"),
        name="moba_flash",
    )(q, kb, vt, kmean, bias)


def _prompt_bias_buckets(blk):
    assert REL_MAX_DIST <= blk + 1
    key = np.arange(blk)[:, None]
    qry = np.arange(blk)[None, :]
    tiles = []
    for dblk in range(3):
        dist = dblk * blk + qry - key
        tiles.append(np.where(dist >= 0, _t5_bucket_np(dist), -1))
    return np.stack(tiles).astype(np.int32)


SAMPLE_BLOCKS_PER_STEP = 8


def _sample_body(pt_ref, bsel_ref, qbdt_ref, *refs, nbp, bps, heads, lq, topk):
    npg = 2 * bps
    k_refs, v_refs = refs[:npg], refs[npg:2 * npg]
    knt_ref, vnt_ref, bias_ref, bias_own_ref, o_ref, m_ref, l_ref, g_ref, acc_ref = refs[2 * npg:]
    step = pl.program_id(1)
    hq = heads * lq
    qbdt = qbdt_ref[0]
    d = qbdt.shape[1]
    hd = d // heads
    lane = lax.broadcasted_iota(jnp.int32, (1, LANES), 1)
    nt = (((1,), (1,)), ((), ()))

    @pl.when(step == 0)
    def _():
        m_ref[...] = jnp.zeros(m_ref.shape, F32)
        l_ref[...] = jnp.zeros(l_ref.shape, F32)
        g_ref[...] = jnp.zeros(g_ref.shape, F32)

    def block(kt, vt, bias):
        sr = jnp.dot(qbdt, kt.astype(BF16), preferred_element_type=F32)
        g = jnp.sum(sr, axis=1, keepdims=True) * (1.0 / kt.shape[1])
        sc = sr + bias
        m = jnp.max(sc, axis=1, keepdims=True)
        p = jnp.exp(sc - m)
        l = jnp.sum(p, axis=1, keepdims=True)
        acc = lax.dot_general(p.astype(BF16), vt.astype(BF16), nt, preferred_element_type=F32)
        slab = jnp.concatenate(
            [acc[h * lq:(h + 1) * lq, (h * hd // LANES) * LANES:(h * hd // LANES + 1) * LANES]
             for h in range(heads)], axis=0)
        return m, l, g, slab

    stats = []
    for jj in range(bps):
        j = step * bps + jj
        kt = jnp.concatenate([k_refs[2 * jj][0], k_refs[2 * jj + 1][0]], axis=1)
        vt = jnp.concatenate([v_refs[2 * jj][0], v_refs[2 * jj + 1][0]], axis=1)
        m, l, g, slab = block(kt, vt, bias_ref[bsel_ref[j]])
        acc_ref[j] = slab
        stats.append((j, m, l, g))
    m_all, l_all, g_all = m_ref[...], l_ref[...], g_ref[...]
    for j, m, l, g in stats:
        hit = lane == j
        m_all = jnp.where(hit, m, m_all)
        l_all = jnp.where(hit, l, l_all)
        g_all = jnp.where(hit, g, g_all)
    m_ref[...] = m_all
    l_ref[...] = l_all
    g_ref[...] = g_all

    @pl.when(step == nbp // bps - 1)
    def _():
        m, l, _, slab = block(knt_ref[0], vnt_ref[0], bias_own_ref[...])
        acc_ref[nbp] = slab
        own = lane == nbp
        m_fin = jnp.where(own, m, m_all)
        l_fin = jnp.where(own, l, l_all)
        lane_f = lane.astype(F32)
        gate = jnp.where(lane < nbp, g_all, NEG_INF)
        sel = jnp.broadcast_to(own, (hq, LANES))
        for _ in range(min(topk, nbp)):
            best = jnp.max(gate, axis=1, keepdims=True)
            first = jnp.min(jnp.where(gate == best, lane_f, float(LANES)), axis=1, keepdims=True)
            pick = lane_f == first
            sel = sel | pick
            gate = jnp.where(pick, NEG_INF, gate)
        top = jnp.max(jnp.where(sel, m_fin, NEG_INF), axis=1, keepdims=True)
        w = jnp.where(sel, jnp.exp(m_fin - top), 0.0)
        wn = w / jnp.sum(w * l_fin, axis=1, keepdims=True)
        out = jnp.zeros((hq, LANES), F32)
        for jj in range(nbp + 1):
            out = out + wn[:, jj:jj + 1] * acc_ref[jj]
        rrow = lax.broadcasted_iota(jnp.int32, (hq, 1), 0)
        keep = (lane // hd) == ((rrow // lq) % (LANES // hd))
        o_ref[0] = jnp.where(keep, out, 0.0)


def _moba_sample(q, k_new, v_new, cache_kt, cache_vt, page_base, page_table, rel_bias):
    b, lq, d = q.shape
    heads = MOBA_HEADS
    hd = d // heads
    hq = heads * lq
    blk = MOBA_BLOCK
    n_pages = page_table.shape[1]
    past = n_pages * PAGE_SIZE
    assert hq == LANES and blk == 2 * PAGE_SIZE and past % blk == 0 and lq <= PAGE_SIZE
    nbp = past // blk
    bps = math.gcd(nbp, SAMPLE_BLOCKS_PER_STEP)
    assert nbp + 1 <= LANES

    qh = q.reshape(b, lq, heads, hd).transpose(0, 2, 1, 3)
    eye = jnp.eye(heads, dtype=F32)
    qbdt = (qh[:, :, :, None, :] * eye[None, :, None, :, None]).reshape(b, hq, d).astype(BF16)

    qry = np.tile(np.arange(lq), heads)[:, None]
    key = np.arange(blk)[None, :]
    tiles, bsel = [], []
    for jb in range(nbp):
        dist = past + qry - (jb * blk + key)
        bk = np.where(dist >= 0, _t5_bucket_np(dist), -1).astype(np.int32)
        for t, existing in enumerate(tiles):
            if np.array_equal(existing, bk):
                bsel.append(t)
                break
        else:
            tiles.append(bk)
            bsel.append(len(tiles) - 1)
    table = jnp.repeat(rel_bias, lq, axis=1)[None, :, :, None]
    bias = _bias_tiles(np.stack(tiles), table)[0]
    dist_own = qry - np.arange(PAGE_SIZE)[None, :]
    own_bk = np.where(dist_own >= 0, _t5_bucket_np(dist_own), -1).astype(np.int32)
    bias_own = _bias_tiles(own_bk[None], table)[0, 0]
    nt = bias.shape[0]

    knt = jnp.pad(k_new.transpose(0, 2, 1), ((0, 0), (0, 0), (0, PAGE_SIZE - lq)))
    vnt = jnp.pad(v_new.transpose(0, 2, 1), ((0, 0), (0, 0), (0, PAGE_SIZE - lq)))
    npg = 2 * bps

    def page(t):
        return lambda bi, s, pt, bs: (page_base + pt[bi * n_pages + s * npg + t], 0, 0)

    per_b = lambda bi, s, pt, bs: (bi, 0, 0)
    page_specs = [pl.BlockSpec((1, d, PAGE_SIZE), page(t)) for t in range(npg)]
    grid_spec = pltpu.PrefetchScalarGridSpec(
        num_scalar_prefetch=2,
        grid=(b, nbp // bps),
        in_specs=[pl.BlockSpec((1, hq, d), per_b)] + page_specs + page_specs + [
            pl.BlockSpec((1, d, PAGE_SIZE), per_b), pl.BlockSpec((1, d, PAGE_SIZE), per_b),
            pl.BlockSpec((nt, hq, blk), lambda bi, s, pt, bs: (0, 0, 0)),
            pl.BlockSpec((hq, PAGE_SIZE), lambda bi, s, pt, bs: (0, 0))],
        out_specs=pl.BlockSpec((1, hq, LANES), per_b),
        scratch_shapes=[pltpu.VMEM((hq, LANES), F32), pltpu.VMEM((hq, LANES), F32),
                        pltpu.VMEM((hq, LANES), F32), pltpu.VMEM((nbp + 1, hq, LANES), F32)],
    )
    o2 = pl.pallas_call(
        functools.partial(_sample_body, nbp=nbp, bps=bps, heads=heads, lq=lq, topk=MOBA_TOPK),
        grid_spec=grid_spec,
        out_shape=jax.ShapeDtypeStruct((b, hq, LANES), F32),
        compiler_params=_params("parallel", "arbitrary"),
        name="moba_sample",
    )(page_table.reshape(-1), jnp.asarray(np.array(bsel, np.int32)), qbdt,
      *([cache_kt] * npg), *([cache_vt] * npg), knt, vnt, bias, bias_own)
    o = o2.reshape(b, heads, lq, LANES // hd, hd).sum(axis=3)
    return o.transpose(0, 2, 1, 3).reshape(b, lq, d).astype(BF16)


def kernel(x_prompt, x_sample, state_ret, cache_k, cache_v, page_table, ffn1_norm, ffn1_w_in, ffn1_w_out,
           mix_norm, ffn2_norm, ffn2_w_in, ffn2_w_out, ret_w_in, ret_gn_gain, ret_w_out, moba_w_qkv,
           moba_w_out, rel_bias, final_norm):
    bp, lp, d = x_prompt.shape
    bs, ls, _ = x_sample.shape
    depth = ffn1_norm.shape[0]
    past = page_table.shape[1] * PAGE_SIZE
    pos_p = jnp.arange(lp, dtype=jnp.int32)
    pos_s = past + jnp.arange(ls, dtype=jnp.int32)
    hp = x_prompt.reshape(bp * lp, d)
    hs = x_sample.reshape(bs * ls, d)
    cast = lambda w: w.astype(BF16)
    ret_p, ret_s, kp_l, vp_l, ks_l, vs_l = [], [], [], [], [], []
    for i in range(depth):
        w1i, w1o = cast(ffn1_w_in[i]), cast(ffn1_w_out[i])
        hp = _ffn(hp, ffn1_norm[i], w1i, w1o)
        hs = _ffn(hs, ffn1_norm[i], w1i, w1o)
        if i % 2 == 0:
            r = i // 2
            w_in, w_out = cast(ret_w_in[r]), cast(ret_w_out[r])
            proj_p = _norm_proj(hp, mix_norm[i], w_in).reshape(bp, lp, -1)
            proj_s = _norm_proj(hs, mix_norm[i], w_in).reshape(bs, ls, -1)
            gp, sp = _retention(proj_p, pos_p, ret_gn_gain[r], None)
            gs, ss = _retention(proj_s, pos_s, ret_gn_gain[r], state_ret[r])
            hp = _proj_res(gp.reshape(bp * lp, -1), w_out, hp)
            hs = _proj_res(gs.reshape(bs * ls, -1), w_out, hs)
            ret_p.append(sp)
            ret_s.append(ss)
        else:
            m = i // 2
            w_qkv, w_out = cast(moba_w_qkv[m]), cast(moba_w_out[m])
            hd = d // MOBA_HEADS
            q, k, v, kb, vt, km = _moba_qkv(hp, mix_norm[i], w_qkv, lp)
            bias = _bias_tiles(_prompt_bias_buckets(MOBA_BLOCK), rel_bias.T[:, :, None, None], LOG2E)
            op = _moba_flash(q.reshape(bp, lp, d), kb.reshape(bp, lp, d), vt,
                             km.reshape(bp, lp // MOBA_BLOCK, d), bias)
            hp = _proj_res(op.reshape(bp * lp, d), w_out, hp)
            kp_l.append(k.reshape(bp, lp, MOBA_HEADS, hd))
            vp_l.append(v.reshape(bp, lp, MOBA_HEADS, hd))

            qs, ksn, vsn = _moba_qkv_small(hs, mix_norm[i], w_qkv)
            n_pool = cache_k.shape[1]
            pages_t = lambda cache: cache.transpose(0, 1, 3, 4, 2).reshape(-1, d, PAGE_SIZE)
            os_ = _moba_sample(qs.reshape(bs, ls, d), ksn.reshape(bs, ls, d), vsn.reshape(bs, ls, d),
                               pages_t(cache_k), pages_t(cache_v), m * n_pool, page_table, rel_bias)
            hs = _proj_res(os_.reshape(bs * ls, d), w_out, hs)
            ks_l.append(ksn.reshape(bs, ls, MOBA_HEADS, hd))
            vs_l.append(vsn.reshape(bs, ls, MOBA_HEADS, hd))
        last = i == depth - 1
        w2i, w2o = cast(ffn2_w_in[i]), cast(ffn2_w_out[i])
        hp = _ffn(hp, ffn2_norm[i], w2i, w2o, final_norm if last else None)
        hs = _ffn(hs, ffn2_norm[i], w2i, w2o, final_norm if last else None)
    return (hp.reshape(bp, lp, d), hs.reshape(bs, ls, d), jnp.stack(ret_p), jnp.stack(ret_s),
            jnp.stack(kp_l), jnp.stack(vp_l), jnp.stack(ks_l), jnp.stack(vs_l))
```

```python
import functools
import math

import jax
import jax.numpy as jnp
import numpy as np
from jax import lax
from jax.experimental import pallas as pl
from jax.experimental.pallas import tpu as pltpu

F32 = jnp.float32
BF16 = jnp.bfloat16
NEG_INF = float("-inf")
LOG2E = math.log2(math.e)

LANES = 128
SUBLANES = 8
VMEM_LIMIT_BYTES = 56 * 1024 * 1024

EPS = 1e-6
ROPE_BASE = 10000.0
RET_HEADS = 4
RET_CHUNK = 128
MOBA_HEADS = 16
MOBA_BLOCK = 256
MOBA_TOPK = 3
PAGE_SIZE = 128
REL_BUCKETS = 32
REL_MAX_DIST = 128


def _params(*semantics):
    return pltpu.CompilerParams(dimension_semantics=semantics, vmem_limit_bytes=VMEM_LIMIT_BYTES)


def _resident(shape):
    zeros = (0,) * len(shape)
    return pl.BlockSpec(shape, lambda *_: zeros, pipeline_mode=pl.Buffered(1))


def _resident_layer(stacked, layer):
    idx = (layer,) + (0,) * (stacked.ndim - 1)
    return pl.BlockSpec((None,) + stacked.shape[1:], lambda *_: idx, pipeline_mode=pl.Buffered(1))


def _row_tile(n, want):
    t = min(n, want)
    assert n % t == 0, (n, t)
    return t


def _rms(x, g):
    y = x * lax.rsqrt(jnp.mean(x * x, axis=-1, keepdims=True) + EPS)
    return y * g


def _ffn_body(*refs, d_ff, tf, final):
    if final:
        x_ref, g_ref, win_ref, wout_ref, fg_ref, o_ref = refs
    else:
        x_ref, g_ref, win_ref, wout_ref, o_ref = refs
    x = x_ref[...]
    h = _rms(x, g_ref[...]).astype(BF16)
    y = None
    for c in range(d_ff // tf):
        a = jnp.dot(h, win_ref[:, c * tf:(c + 1) * tf], preferred_element_type=F32)
        b = jnp.dot(h, win_ref[:, d_ff + c * tf:d_ff + (c + 1) * tf], preferred_element_type=F32)
        gated = (a * jax.nn.sigmoid(a) * b).astype(BF16)
        part = jnp.dot(gated, wout_ref[c * tf:(c + 1) * tf, :], preferred_element_type=F32)
        y = part if y is None else y + part
    out = x + 0.5 * y
    if final:
        out = _rms(out, fg_ref[...])
    o_ref[...] = out


MXU_TILE = 256


def _ffn(x, g, w_in, w_out, layer, final_g=None, *, tm=512):
    n, d = x.shape
    d_ff = w_out.shape[1]
    tm = _row_tile(n, tm)
    tf = d_ff if d_ff % MXU_TILE == 0 else LANES
    assert d_ff % tf == 0
    final = final_g is not None
    ins = [x, g.reshape(1, d), w_in, w_out]
    specs = [pl.BlockSpec((tm, d), lambda i: (i, 0)), _resident((1, d)),
             _resident_layer(w_in, layer), _resident_layer(w_out, layer)]
    if final:
        ins.append(final_g.reshape(1, d))
        specs.append(_resident((1, d)))
    return pl.pallas_call(
        functools.partial(_ffn_body, d_ff=d_ff, tf=tf, final=final),
        grid=(n // tm,),
        in_specs=specs,
        out_specs=pl.BlockSpec((tm, d), lambda i: (i, 0)),
        out_shape=jax.ShapeDtypeStruct((n, d), F32),
        compiler_params=_params("parallel"),
        name="ffn",
    )(*ins)


def _norm_proj_body(x_ref, g_ref, w_ref, o_ref):
    h = _rms(x_ref[...], g_ref[...]).astype(BF16)
    o_ref[...] = jnp.dot(h, w_ref[...], preferred_element_type=F32)


def _norm_proj(x, g, w, *, tm=256):
    n, d = x.shape
    nout = w.shape[1]
    tm = _row_tile(n, tm)
    return pl.pallas_call(
        _norm_proj_body,
        grid=(n // tm,),
        in_specs=[pl.BlockSpec((tm, d), lambda i: (i, 0)), _resident((1, d)), _resident(w.shape)],
        out_specs=pl.BlockSpec((tm, nout), lambda i: (i, 0)),
        out_shape=jax.ShapeDtypeStruct((n, nout), F32),
        compiler_params=_params("parallel"),
        name="norm_proj",
    )(x, g.reshape(1, d), w)


def _proj_res_body(x_ref, w_ref, r_ref, o_ref):
    o_ref[...] = r_ref[...] + jnp.dot(x_ref[...], w_ref[...], preferred_element_type=F32)


def _proj_res(x, w, res, *, tm=512):
    n, k = x.shape
    d = w.shape[1]
    tm = _row_tile(n, tm)
    return pl.pallas_call(
        _proj_res_body,
        grid=(n // tm,),
        in_specs=[pl.BlockSpec((tm, k), lambda i: (i, 0)), _resident(w.shape),
                  pl.BlockSpec((tm, d), lambda i: (i, 0))],
        out_specs=pl.BlockSpec((tm, d), lambda i: (i, 0)),
        out_shape=jax.ShapeDtypeStruct((n, d), F32),
        compiler_params=_params("parallel"),
        name="proj_res",
    )(x, w, res)


def _ret_body(*refs, nh, dk, dv, c, cp, has_s0):
    if has_s0:
        (q_ref, k_ref, v_ref, g_ref, cos_ref, sin_ref, din_ref, dq_ref, dkk_ref, dc_ref,
         gain_ref, s0_ref, o_ref, s_ref) = refs
    else:
        (q_ref, k_ref, v_ref, g_ref, cos_ref, sin_ref, din_ref, dq_ref, dkk_ref, dc_ref,
         gain_ref, o_ref, s_ref) = refs
    i = pl.program_id(1)

    @pl.when(i == 0)
    def _():
        if has_s0:
            s_ref[...] = s0_ref[...]
        else:
            s_ref[...] = jnp.zeros(s_ref.shape, F32)

    def pad(t):
        if cp == c:
            return t
        return jnp.concatenate([t, jnp.zeros((cp - c, t.shape[1]), t.dtype)], axis=0)

    cos = pad(cos_ref[...])
    sin = pad(sin_ref[...])
    half = dk // 2

    def rope(t):
        t1, t2 = t[:, :half], t[:, half:]
        return jnp.concatenate([t1 * cos - t2 * sin, t1 * sin + t2 * cos], axis=-1)

    for h in range(nh):
        qr = rope(pad(q_ref[0, :, h * dk:(h + 1) * dk]))
        kr = rope(pad(k_ref[0, :, h * dk:(h + 1) * dk])) * (dk ** -0.5)
        qb = qr.astype(BF16)
        kb = kr.astype(BF16)
        vb = pad(v_ref[0, :, h * dv:(h + 1) * dv]).astype(BF16)
        att = lax.dot_general(qb, kb, (((1,), (1,)), ((), ())), preferred_element_type=F32) * din_ref[h]
        o_in = jnp.dot(att.astype(BF16), vb, preferred_element_type=F32)
        s = s_ref[0, h]
        o_x = jnp.dot(qb, s.astype(BF16), preferred_element_type=F32) * dq_ref[h]
        kd = (kr * dkk_ref[h]).T.astype(BF16)
        s_ref[0, h] = s * dc_ref[h] + jnp.dot(kd, vb, preferred_element_type=F32)
        o = (o_in + o_x)[:c]
        mu = jnp.mean(o, axis=-1, keepdims=True)
        d = o - mu
        var = jnp.mean(d * d, axis=-1, keepdims=True)
        on = d * lax.rsqrt(var + EPS) * gain_ref[:, h * dv:(h + 1) * dv]
        gt = g_ref[0, :, h * dv:(h + 1) * dv]
        o_ref[0, :, h * dv:(h + 1) * dv] = (gt * jax.nn.sigmoid(gt) * on).astype(BF16)


def _retention(proj, pos, gn_gain, s0):
    b, l, width = proj.shape
    nh = RET_HEADS
    hk = width // 6
    hv = 2 * hk
    dk, dv = hk // nh, hv // nh
    c = math.gcd(l, RET_CHUNK)
    cp = RET_CHUNK
    n = l // c
    half = dk // 2
    inv = ROPE_BASE ** (-jnp.arange(half, dtype=F32) / half)
    ang = pos.astype(F32)[:, None] * inv[None, :]
    cos, sin = jnp.cos(ang), jnp.sin(ang)
    log_g = jnp.log1p(-jnp.exp2(-5.0 - jnp.arange(nh, dtype=F32)))
    idx = jnp.arange(c, dtype=F32)
    rel = idx[:, None] - idx[None, :]
    d_in = jnp.where(rel[None] >= 0, jnp.exp(log_g[:, None, None] * jnp.maximum(rel, 0.0)[None]), 0.0)
    d_q = jnp.exp(log_g[:, None] * (idx[None, :] + 1.0))
    d_k = jnp.exp(log_g[:, None] * (c - 1.0 - idx[None, :]))
    d_c = jnp.exp(log_g * c)
    d_in = jnp.pad(d_in, ((0, 0), (0, cp - c), (0, cp - c)))
    d_q = jnp.pad(d_q, ((0, 0), (0, cp - c)))[:, :, None]
    d_k = jnp.pad(d_k, ((0, 0), (0, cp - c)))[:, :, None]
    d_c = d_c[:, None, None]

    has_s0 = s0 is not None
    ins = [proj, proj, proj, proj, cos, sin, d_in, d_q, d_k, d_c, gn_gain.reshape(1, hv)]
    specs = [
        pl.BlockSpec((1, c, hk), lambda bi, i: (bi, i, 0)),
        pl.BlockSpec((1, c, hk), lambda bi, i: (bi, i, 1)),
        pl.BlockSpec((1, c, hv), lambda bi, i: (bi, i, 1)),
        pl.BlockSpec((1, c, hv), lambda bi, i: (bi, i, 2)),
        pl.BlockSpec((c, half), lambda bi, i: (i, 0)),
        pl.BlockSpec((c, half), lambda bi, i: (i, 0)),
        _resident(d_in.shape), _resident(d_q.shape), _resident(d_k.shape), _resident(d_c.shape),
        _resident((1, hv)),
    ]
    if has_s0:
        ins.append(s0)
        specs.append(pl.BlockSpec((1, nh, dk, dv), lambda bi, i: (bi, 0, 0, 0)))
    return pl.pallas_call(
        functools.partial(_ret_body, nh=nh, dk=dk, dv=dv, c=c, cp=cp, has_s0=has_s0),
        grid=(b, n),
        in_specs=specs,
        out_specs=[pl.BlockSpec((1, c, hv), lambda bi, i: (bi, i, 0)),
                   pl.BlockSpec((1, nh, dk, dv), lambda bi, i: (bi, 0, 0, 0))],
        out_shape=[jax.ShapeDtypeStruct((b, l, hv), BF16),
                   jax.ShapeDtypeStruct((b, nh, dk, dv), F32)],
        compiler_params=_params("parallel", "arbitrary"),
        name="retention",
    )(*ins)


def _qkv_body(x_ref, g_ref, w_ref, q_ref, k_ref, v_ref, kb_ref, vt_ref, km_ref, *, d, blk, scale):
    h = _rms(x_ref[...], g_ref[...]).astype(BF16)
    y = jnp.dot(h, w_ref[...], preferred_element_type=F32)
    k = y[:, d:2 * d]
    v = y[:, 2 * d:]
    q_ref[...] = (y[:, :d] * scale).astype(BF16)
    k_ref[...] = k.reshape(k_ref.shape)
    v_ref[...] = v.reshape(v_ref.shape)
    kb_ref[...] = k.astype(BF16)
    vt_ref[0] = v.T.astype(BF16)
    for r in range(k.shape[0] // blk):
        km_ref[r] = jnp.sum(k[r * blk:(r + 1) * blk], axis=0, keepdims=True) * (1.0 / blk)


def _moba_qkv(x, g, w, seq, *, tm=256):
    n, d = x.shape
    hd = d // MOBA_HEADS
    blk = MOBA_BLOCK
    tm = _row_tile(n, tm)
    assert tm % blk == 0 and seq % tm == 0
    per_seq = seq // tm
    row = lambda i: (i, 0)
    return pl.pallas_call(
        functools.partial(_qkv_body, d=d, blk=blk, scale=LOG2E * hd ** -0.5),
        grid=(n // tm,),
        in_specs=[pl.BlockSpec((tm, d), row), _resident((1, d)), _resident(w.shape)],
        out_specs=[pl.BlockSpec((tm, d), row),
                   pl.BlockSpec((tm, MOBA_HEADS, hd), lambda i: (i, 0, 0)),
                   pl.BlockSpec((tm, MOBA_HEADS, hd), lambda i: (i, 0, 0)),
                   pl.BlockSpec((tm, d), row),
                   pl.BlockSpec((1, d, tm), lambda i: (i // per_seq, 0, i % per_seq)),
                   pl.BlockSpec((tm // blk, 1, d), lambda i: (i, 0, 0))],
        out_shape=[jax.ShapeDtypeStruct((n, d), BF16), jax.ShapeDtypeStruct((n, MOBA_HEADS, hd), F32),
                   jax.ShapeDtypeStruct((n, MOBA_HEADS, hd), F32), jax.ShapeDtypeStruct((n, d), BF16),
                   jax.ShapeDtypeStruct((n // seq, d, seq), BF16),
                   jax.ShapeDtypeStruct((n // blk, 1, d), F32)],
        compiler_params=_params("parallel"),
        name="moba_qkv",
    )(x, g.reshape(1, d), w)


def _qkv_small_body(x_ref, g_ref, w_ref, q_ref, k_ref, v_ref, *, d, scale):
    h = _rms(x_ref[...], g_ref[...]).astype(BF16)
    y = jnp.dot(h, w_ref[...], preferred_element_type=F32)
    q_ref[...] = y[:, :d] * scale
    k_ref[...] = y[:, d:2 * d]
    v_ref[...] = y[:, 2 * d:]


def _moba_qkv_small(x, g, w):
    n, d = x.shape
    hd = d // MOBA_HEADS
    full = pl.BlockSpec((n, d), lambda i: (0, 0))
    return pl.pallas_call(
        functools.partial(_qkv_small_body, d=d, scale=hd ** -0.5),
        grid=(1,),
        in_specs=[full, _resident((1, d)), _resident(w.shape)],
        out_specs=[full, full, full],
        out_shape=[jax.ShapeDtypeStruct((n, d), F32)] * 3,
        compiler_params=_params("arbitrary"),
        name="moba_qkv_small",
    )(x, g.reshape(1, d), w)


def _t5_bucket_np(dist):
    n = np.maximum(dist, 0)
    max_exact = REL_BUCKETS // 2
    nf = np.maximum(n, 1).astype(np.float64)
    large = max_exact + (np.log(nf / max_exact) / math.log(REL_MAX_DIST / max_exact)
                         * (REL_BUCKETS - max_exact)).astype(np.int64)
    large = np.minimum(large, REL_BUCKETS - 1)
    return np.where(n < max_exact, n, large).astype(np.int32)


def _bias_body(bk_ref, tab_ref, o_ref, *, scale):
    bk = bk_ref[0]
    acc = jnp.where(bk < 0, NEG_INF, 0.0).astype(F32)
    for b in range(REL_BUCKETS):
        acc = jnp.where(bk == b, tab_ref[0, b] * scale, acc)
    o_ref[0, 0] = acc


def _bias_tiles(buckets, table, scale=1.0):
    t, r, w = buckets.shape
    g, nbuck, tr, tw = table.shape
    assert nbuck == REL_BUCKETS and tr in (1, r) and tw in (1, w)
    return pl.pallas_call(
        functools.partial(_bias_body, scale=scale),
        grid=(g, t),
        in_specs=[pl.BlockSpec((1, r, w), lambda gi, ti: (ti, 0, 0)),
                  pl.BlockSpec((1, REL_BUCKETS, tr, tw), lambda gi, ti: (gi, 0, 0, 0))],
        out_specs=pl.BlockSpec((1, 1, r, w), lambda gi, ti: (gi, ti, 0, 0)),
        out_shape=jax.ShapeDtypeStruct((g, t, r, w), F32),
        compiler_params=_params("parallel", "parallel"),
        name="t5_bias_tiles",
    )(jnp.asarray(buckets), table)


FLASH_HEADS = 8
DEN_ROWS = 16


def _flash_body(q_ref, k_ref, vt_ref, km_ref, bias_ref, o_ref, *, nb, blk, hd, topk, nch):
    pair = 2 * hd
    lane = lax.broadcasted_iota(jnp.int32, (1, pair), 1)
    row = lax.broadcasted_iota(jnp.int32, (pair, 1), 0)
    bidx = lax.broadcasted_iota(jnp.int32, (nb, 1), 0)
    nt = (((1,), (1,)), ((), ()))

    wq = 2 * blk

    def lanes(pc):
        return slice(pc * pair, (pc + 1) * pair)

    def both(pc, idx):
        return jnp.concatenate([bias_ref[(2 * pc,) + idx], bias_ref[(2 * pc + 1,) + idx]], axis=1)

    def masked_q(pc):
        qp = q_ref[0, :, lanes(pc)]
        zero = jnp.zeros((blk, pair), BF16)
        return jnp.concatenate([jnp.where(lane < hd, qp, zero), jnp.where(lane >= hd, qp, zero)], axis=0)

    def q_block(t):
        npair = nch // 2
        qs = [masked_q(pc) for pc in range(npair)]
        adds = []
        for pc in range(npair):
            add = [None] * (t + 1)
            add[t] = both(pc, (0,))
            if t >= 1:
                gate = lax.dot_general(km_ref[0, :, lanes(pc)].astype(BF16), qs[pc], nt,
                                       preferred_element_type=F32)
                valid = bidx < t
                gate = jnp.where(valid, gate, NEG_INF)
                rank = jnp.zeros((nb, wq), jnp.int32)
                for j2 in range(t):
                    r2 = gate[j2:j2 + 1, :]
                    ahead = (r2 > gate) | ((r2 == gate) & (j2 < bidx))
                    rank = rank + ahead.astype(jnp.int32)
                sel = valid & (rank < topk)
                add[t - 1] = both(pc, (1,)) + jnp.where(sel[t - 1:t], 0.0, NEG_INF)
                far = jnp.where(sel, both(pc, (2, slice(0, 1))), NEG_INF)
                for j in range(t - 1):
                    add[j] = far[j:j + 1]
            adds.append(add)

        order = [t] + list(range(t))

        def qk(pc, j):
            return lax.dot_general(k_ref[0, j * blk:(j + 1) * blk, lanes(pc)], qs[pc], nt,
                                   preferred_element_type=F32)

        ones = jnp.ones((DEN_ROWS, blk), BF16)

        def pv(pc, j, p):
            vt1 = jnp.concatenate([vt_ref[0, lanes(pc), j * blk:(j + 1) * blk], ones], axis=0)
            return jnp.dot(vt1, p, preferred_element_type=F32)

        state = [dict(x={}, m=None, acc=None) for _ in range(npair)]

        def pass1(pc):
            st = state[pc]
            pend = None
            for j in order:
                s = qk(pc, j)
                yield
                if pend is not None:
                    finish(st, pc, *pend)
                pend = (j, s)
            finish(st, pc, *pend)
            yield

        def finish(st, pc, j, s):
            add = adds[pc][j]
            if add.shape[0] == 1:
                st["x"][j] = s
                mx = s.max(axis=0, keepdims=True) + add
            else:
                st["x"][j] = s + add
                mx = st["x"][j].max(axis=0, keepdims=True)
            st["m"] = mx if st["m"] is None else jnp.maximum(st["m"], mx)

        def pass2(pc):
            st = state[pc]
            pend = None
            for j in order:
                add = adds[pc][j]
                shift = st["m"] - add if add.shape[0] == 1 else st["m"]
                p = jnp.exp2(st["x"][j] - shift).astype(BF16)
                yield
                if pend is not None:
                    a = pv(pc, *pend)
                    st["acc"] = a if st["acc"] is None else st["acc"] + a
                pend = (j, p)
            a = pv(pc, *pend)
            st["acc"] = a if st["acc"] is None else st["acc"] + a
            yield

        def run_together(*gens):
            gens = list(gens)
            while gens:
                for g in list(gens):
                    try:
                        next(g)
                    except StopIteration:
                        gens.remove(g)

        run_together(pass1(0))
        for pc in range(npair):
            if pc + 1 < npair:
                run_together(pass1(pc + 1), pass2(pc))
            else:
                run_together(pass2(pc))
        for pc in range(npair):
            st = state[pc]
            acc = st["acc"][:pair] / st["acc"][pair:pair + 1]
            out_t = jnp.where(row >= hd, acc[:, blk:], acc[:, :blk])
            o_ref[0, :, lanes(pc)] = out_t.T.astype(BF16)

    for t in range(nb):
        pl.when(pl.program_id(2) == t)(functools.partial(q_block, t))


def _moba_flash(q, kb, vt, kmean, bias):
    b, l, d = q.shape
    hd = d // MOBA_HEADS
    pair = 2 * hd
    assert pair == LANES
    blk = MOBA_BLOCK
    nb = l // blk
    nch = FLASH_HEADS
    w = nch * hd
    assert nch % 2 == 0 and d % w == 0
    return pl.pallas_call(
        functools.partial(_flash_body, nb=nb, blk=blk, hd=hd, topk=MOBA_TOPK, nch=nch),
        grid=(b, d // w, nb),
        in_specs=[pl.BlockSpec((1, blk, w), lambda bi, p, qi: (bi, qi, p)),
                  pl.BlockSpec((1, l, w), lambda bi, p, qi: (bi, 0, p)),
                  pl.BlockSpec((1, w, l), lambda bi, p, qi: (bi, p, 0)),
                  pl.BlockSpec((1, nb, w), lambda bi, p, qi: (bi, 0, p)),
                  pl.BlockSpec((nch, 3, blk, blk), lambda bi, p, qi: (p, 0, 0, 0))],
        out_specs=pl.BlockSpec((1, blk, w), lambda bi, p, qi: (bi, qi, p)),
        out_shape=jax.ShapeDtypeStruct((b, l, d), BF16),
        compiler_params=_params("parallel", "parallel", "arbitrary"),
        name="moba_flash",
    )(q, kb, vt, kmean, bias)


def _prompt_bias_buckets(blk):
    assert REL_MAX_DIST <= blk + 1
    key = np.arange(blk)[:, None]
    qry = np.arange(blk)[None, :]
    tiles = []
    for dblk in range(3):
        dist = dblk * blk + qry - key
        tiles.append(np.where(dist >= 0, _t5_bucket_np(dist), -1))
    return np.stack(tiles).astype(np.int32)


SAMPLE_BLOCKS_PER_STEP = 8


def _sample_body(pt_ref, bsel_ref, qbdt_ref, *refs, nbp, bps, heads, lq, topk):
    npg = 2 * bps
    k_refs, v_refs = refs[:npg], refs[npg:2 * npg]
    kn_ref, vn_ref, bias_ref, bias_own_ref, o_ref, m_ref, l_ref, g_ref, acc_ref = refs[2 * npg:]
    step = pl.program_id(1)
    hq = heads * lq
    qbdt = qbdt_ref[0]
    d = qbdt.shape[1]
    hd = d // heads
    lane = lax.broadcasted_iota(jnp.int32, (1, LANES), 1)
    nt = (((1,), (1,)), ((), ()))

    @pl.when(step == 0)
    def _():
        m_ref[...] = jnp.zeros(m_ref.shape, F32)
        l_ref[...] = jnp.zeros(l_ref.shape, F32)
        g_ref[...] = jnp.zeros(g_ref.shape, F32)

    def block(k, v, bias, token_major=False):
        kb, vb = k.astype(BF16), v.astype(BF16)
        if token_major:
            sr = lax.dot_general(qbdt, kb, nt, preferred_element_type=F32)
        else:
            sr = jnp.dot(qbdt, kb, preferred_element_type=F32)
        g = jnp.sum(sr, axis=1, keepdims=True) * (1.0 / sr.shape[1])
        sc = sr + bias
        m = jnp.max(sc, axis=1, keepdims=True)
        p = jnp.exp(sc - m)
        l = jnp.sum(p, axis=1, keepdims=True)
        if token_major:
            acc = jnp.dot(p.astype(BF16), vb, preferred_element_type=F32)
        else:
            acc = lax.dot_general(p.astype(BF16), vb, nt, preferred_element_type=F32)
        slab = jnp.concatenate(
            [acc[h * lq:(h + 1) * lq, (h * hd // LANES) * LANES:(h * hd // LANES + 1) * LANES]
             for h in range(heads)], axis=0)
        return m, l, g, slab

    stats = []
    for jj in range(bps):
        j = step * bps + jj
        kt = jnp.concatenate([k_refs[2 * jj][0], k_refs[2 * jj + 1][0]], axis=1)
        vt = jnp.concatenate([v_refs[2 * jj][0], v_refs[2 * jj + 1][0]], axis=1)
        m, l, g, slab = block(kt, vt, bias_ref[bsel_ref[j]])
        acc_ref[j] = slab
        stats.append((j, m, l, g))
    m_all, l_all, g_all = m_ref[...], l_ref[...], g_ref[...]
    for j, m, l, g in stats:
        hit = lane == j
        m_all = jnp.where(hit, m, m_all)
        l_all = jnp.where(hit, l, l_all)
        g_all = jnp.where(hit, g, g_all)
    m_ref[...] = m_all
    l_ref[...] = l_all
    g_ref[...] = g_all

    @pl.when(step == nbp // bps - 1)
    def _():
        zpad = jnp.zeros((PAGE_SIZE - lq, d), F32)
        m, l, _, slab = block(jnp.concatenate([kn_ref[0], zpad], axis=0),
                              jnp.concatenate([vn_ref[0], zpad], axis=0), bias_own_ref[...], token_major=True)
        acc_ref[nbp] = slab
        own = lane == nbp
        m_fin = jnp.where(own, m, m_all)
        l_fin = jnp.where(own, l, l_all)
        lane_f = lane.astype(F32)
        gate = jnp.where(lane < nbp, g_all, NEG_INF)
        sel = jnp.broadcast_to(own, (hq, LANES))
        for _ in range(min(topk, nbp)):
            best = jnp.max(gate, axis=1, keepdims=True)
            first = jnp.min(jnp.where(gate == best, lane_f, float(LANES)), axis=1, keepdims=True)
            pick = lane_f == first
            sel = sel | pick
            gate = jnp.where(pick, NEG_INF, gate)
        top = jnp.max(jnp.where(sel, m_fin, NEG_INF), axis=1, keepdims=True)
        w = jnp.where(sel, jnp.exp(m_fin - top), 0.0)
        wn = w / jnp.sum(w * l_fin, axis=1, keepdims=True)
        out = jnp.zeros((hq, LANES), F32)
        for jj in range(nbp + 1):
            out = out + wn[:, jj:jj + 1] * acc_ref[jj]
        rrow = lax.broadcasted_iota(jnp.int32, (hq, 1), 0)
        keep = (lane // hd) == ((rrow // lq) % (LANES // hd))
        o_ref[0] = jnp.where(keep, out, 0.0)


def _moba_sample(q, k_new, v_new, cache_kt, cache_vt, page_base, page_table, rel_bias):
    b, lq, d = q.shape
    heads = MOBA_HEADS
    hd = d // heads
    hq = heads * lq
    blk = MOBA_BLOCK
    n_pages = page_table.shape[1]
    past = n_pages * PAGE_SIZE
    assert hq == LANES and blk == 2 * PAGE_SIZE and past % blk == 0 and lq <= PAGE_SIZE
    nbp = past // blk
    bps = math.gcd(nbp, SAMPLE_BLOCKS_PER_STEP)
    assert nbp + 1 <= LANES

    same_head = (np.arange(hq)[:, None] // lq) == (np.arange(d)[None, :] // hd)
    qbdt = jnp.where(jnp.asarray(same_head), jnp.tile(q, (1, heads, 1)), 0.0).astype(BF16)

    qry = np.tile(np.arange(lq), heads)[:, None]
    key = np.arange(blk)[None, :]
    tiles, bsel = [], []
    for jb in range(nbp):
        dist = past + qry - (jb * blk + key)
        bk = np.where(dist >= 0, _t5_bucket_np(dist), -1).astype(np.int32)
        for t, existing in enumerate(tiles):
            if np.array_equal(existing, bk):
                bsel.append(t)
                break
        else:
            tiles.append(bk)
            bsel.append(len(tiles) - 1)
    table = jnp.repeat(rel_bias, lq, axis=1)[None, :, :, None]
    bias = _bias_tiles(np.stack(tiles), table)[0]
    dist_own = qry - np.arange(PAGE_SIZE)[None, :]
    own_bk = np.where(dist_own >= 0, _t5_bucket_np(dist_own), -1).astype(np.int32)
    bias_own = _bias_tiles(own_bk[None], table)[0, 0]
    nt = bias.shape[0]

    npg = 2 * bps

    def page(t):
        return lambda bi, s, pt, bs: (page_base + pt[bi * n_pages + s * npg + t], 0, 0)

    per_b = lambda bi, s, pt, bs: (bi, 0, 0)
    page_specs = [pl.BlockSpec((1, d, PAGE_SIZE), page(t)) for t in range(npg)]
    grid_spec = pltpu.PrefetchScalarGridSpec(
        num_scalar_prefetch=2,
        grid=(b, nbp // bps),
        in_specs=[pl.BlockSpec((1, hq, d), per_b)] + page_specs + page_specs + [
            pl.BlockSpec((1, lq, d), per_b), pl.BlockSpec((1, lq, d), per_b),
            pl.BlockSpec((nt, hq, blk), lambda bi, s, pt, bs: (0, 0, 0)),
            pl.BlockSpec((hq, PAGE_SIZE), lambda bi, s, pt, bs: (0, 0))],
        out_specs=pl.BlockSpec((1, hq, LANES), per_b),
        scratch_shapes=[pltpu.VMEM((hq, LANES), F32), pltpu.VMEM((hq, LANES), F32),
                        pltpu.VMEM((hq, LANES), F32), pltpu.VMEM((nbp + 1, hq, LANES), F32)],
    )
    o2 = pl.pallas_call(
        functools.partial(_sample_body, nbp=nbp, bps=bps, heads=heads, lq=lq, topk=MOBA_TOPK),
        grid_spec=grid_spec,
        out_shape=jax.ShapeDtypeStruct((b, hq, LANES), F32),
        compiler_params=_params("parallel", "arbitrary"),
        name="moba_sample",
    )(page_table.reshape(-1), jnp.asarray(np.array(bsel, np.int32)), qbdt,
      *([cache_kt] * npg), *([cache_vt] * npg), k_new, v_new, bias, bias_own)
    o = o2.reshape(b, heads, lq, LANES // hd, hd).sum(axis=3)
    return o.transpose(0, 2, 1, 3).reshape(b, lq, d).astype(BF16)


def kernel(x_prompt, x_sample, state_ret, cache_k, cache_v, page_table, ffn1_norm, ffn1_w_in, ffn1_w_out,
           mix_norm, ffn2_norm, ffn2_w_in, ffn2_w_out, ret_w_in, ret_gn_gain, ret_w_out, moba_w_qkv,
           moba_w_out, rel_bias, final_norm):
    bp, lp, d = x_prompt.shape
    bs, ls, _ = x_sample.shape
    depth = ffn1_norm.shape[0]
    past = page_table.shape[1] * PAGE_SIZE
    pos_p = jnp.arange(lp, dtype=jnp.int32)
    pos_s = past + jnp.arange(ls, dtype=jnp.int32)
    hp = x_prompt.reshape(bp * lp, d)
    hs = x_sample.reshape(bs * ls, d)
    cast = lambda w: w.astype(BF16)
    w1i, w1o, w2i, w2o = cast(ffn1_w_in), cast(ffn1_w_out), cast(ffn2_w_in), cast(ffn2_w_out)
    ret_p, ret_s, kp_l, vp_l, ks_l, vs_l = [], [], [], [], [], []
    for i in range(depth):
        hp = _ffn(hp, ffn1_norm[i], w1i, w1o, i)
        hs = _ffn(hs, ffn1_norm[i], w1i, w1o, i)
        if i % 2 == 0:
            r = i // 2
            w_in, w_out = cast(ret_w_in[r]), cast(ret_w_out[r])
            proj_p = _norm_proj(hp, mix_norm[i], w_in).reshape(bp, lp, -1)
            proj_s = _norm_proj(hs, mix_norm[i], w_in).reshape(bs, ls, -1)
            gp, sp = _retention(proj_p, pos_p, ret_gn_gain[r], None)
            gs, ss = _retention(proj_s, pos_s, ret_gn_gain[r], state_ret[r])
            hp = _proj_res(gp.reshape(bp * lp, -1), w_out, hp)
            hs = _proj_res(gs.reshape(bs * ls, -1), w_out, hs)
            ret_p.append(sp)
            ret_s.append(ss)
        else:
            m = i // 2
            w_qkv, w_out = cast(moba_w_qkv[m]), cast(moba_w_out[m])
            hd = d // MOBA_HEADS
            q, k, v, kb, vt, km = _moba_qkv(hp, mix_norm[i], w_qkv, lp)
            bias = _bias_tiles(_prompt_bias_buckets(MOBA_BLOCK), rel_bias.T[:, :, None, None], LOG2E)
            op = _moba_flash(q.reshape(bp, lp, d), kb.reshape(bp, lp, d), vt,
                             km.reshape(bp, lp // MOBA_BLOCK, d), bias)
            hp = _proj_res(op.reshape(bp * lp, d), w_out, hp)
            kp_l.append(k.reshape(bp, lp, MOBA_HEADS, hd))
            vp_l.append(v.reshape(bp, lp, MOBA_HEADS, hd))

            qs, ksn, vsn = _moba_qkv_small(hs, mix_norm[i], w_qkv)
            n_pool = cache_k.shape[1]
            pages_t = lambda cache: cache.transpose(0, 1, 3, 4, 2).reshape(-1, d, PAGE_SIZE)
            os_ = _moba_sample(qs.reshape(bs, ls, d), ksn.reshape(bs, ls, d), vsn.reshape(bs, ls, d),
                               pages_t(cache_k), pages_t(cache_v), m * n_pool, page_table, rel_bias)
            hs = _proj_res(os_.reshape(bs * ls, d), w_out, hs)
            ks_l.append(ksn.reshape(bs, ls, MOBA_HEADS, hd))
            vs_l.append(vsn.reshape(bs, ls, MOBA_HEADS, hd))
        last = i == depth - 1
        hp = _ffn(hp, ffn2_norm[i], w2i, w2o, i, final_norm if last else None)
        hs = _ffn(hs, ffn2_norm[i], w2i, w2o, i, final_norm if last else None)
    return (hp.reshape(bp, lp, d), hs.reshape(bs, ls, d), jnp.stack(ret_p), jnp.stack(ret_s),
            jnp.stack(kp_l), jnp.stack(vp_l), jnp.stack(ks_l), jnp.stack(vs_l))
```

```python
import functools
import math

import jax
import jax.numpy as jnp
import numpy as np
from jax import lax
from jax.experimental import pallas as pl
from jax.experimental.pallas import tpu as pltpu

F32 = jnp.float32
BF16 = jnp.bfloat16
NEG_INF = float("-inf")
LOG2E = math.log2(math.e)

LANES = 128
SUBLANES = 8
VMEM_LIMIT_BYTES = 56 * 1024 * 1024

EPS = 1e-6
ROPE_BASE = 10000.0
RET_HEADS = 4
RET_CHUNK = 128
MOBA_HEADS = 16
MOBA_BLOCK = 256
MOBA_TOPK = 3
PAGE_SIZE = 128
REL_BUCKETS = 32
REL_MAX_DIST = 128


def _params(*semantics):
    return pltpu.CompilerParams(dimension_semantics=semantics, vmem_limit_bytes=VMEM_LIMIT_BYTES)


def _resident(shape):
    zeros = (0,) * len(shape)
    return pl.BlockSpec(shape, lambda *_: zeros, pipeline_mode=pl.Buffered(1))


def _resident_layer(stacked, layer):
    idx = (layer,) + (0,) * (stacked.ndim - 1)
    return pl.BlockSpec((None,) + stacked.shape[1:], lambda *_: idx, pipeline_mode=pl.Buffered(1))


def _row_tile(n, want):
    t = min(n, want)
    assert n % t == 0, (n, t)
    return t


def _rms(x, g):
    y = x * lax.rsqrt(jnp.mean(x * x, axis=-1, keepdims=True) + EPS)
    return y * g


def _ffn_body(*refs, d_ff, tf, final):
    if final:
        x_ref, g_ref, win_ref, wout_ref, fg_ref, o_ref = refs
    else:
        x_ref, g_ref, win_ref, wout_ref, o_ref = refs
    x = x_ref[...]
    h = _rms(x, g_ref[...]).astype(BF16)
    y = None
    for c in range(d_ff // tf):
        a = jnp.dot(h, win_ref[:, c * tf:(c + 1) * tf], preferred_element_type=F32)
        b = jnp.dot(h, win_ref[:, d_ff + c * tf:d_ff + (c + 1) * tf], preferred_element_type=F32)
        gated = (a * jax.nn.sigmoid(a) * b).astype(BF16)
        part = jnp.dot(gated, wout_ref[c * tf:(c + 1) * tf, :], preferred_element_type=F32)
        y = part if y is None else y + part
    out = x + 0.5 * y
    if final:
        out = _rms(out, fg_ref[...])
    o_ref[...] = out


MXU_TILE = 256


def _ffn(x, g, w_in, w_out, layer, final_g=None, *, tm=512):
    n, d = x.shape
    d_ff = w_out.shape[1]
    tm = _row_tile(n, tm)
    tf = d_ff if d_ff % MXU_TILE == 0 else LANES
    assert d_ff % tf == 0
    final = final_g is not None
    ins = [x, g.reshape(1, d), w_in, w_out]
    specs = [pl.BlockSpec((tm, d), lambda i: (i, 0)), _resident((1, d)),
             _resident_layer(w_in, layer), _resident_layer(w_out, layer)]
    if final:
        ins.append(final_g.reshape(1, d))
        specs.append(_resident((1, d)))
    return pl.pallas_call(
        functools.partial(_ffn_body, d_ff=d_ff, tf=tf, final=final),
        grid=(n // tm,),
        in_specs=specs,
        out_specs=pl.BlockSpec((tm, d), lambda i: (i, 0)),
        out_shape=jax.ShapeDtypeStruct((n, d), F32),
        compiler_params=_params("parallel"),
        name="ffn",
    )(*ins)


def _ret_proj_body(x_ref, g_ref, w_ref, cos_ref, sin_ref, o_ref, *, nh, dk, dv):
    h = _rms(x_ref[...], g_ref[...]).astype(BF16)
    y = jnp.dot(h, w_ref[...], preferred_element_type=F32)
    cos, sin = cos_ref[...], sin_ref[...]
    half, hk, hv = dk // 2, nh * dk, nh * dv
    for part in range(2):
        for hh in range(nh):
            off = part * hk + hh * dk
            t1, t2 = y[:, off:off + half], y[:, off + half:off + dk]
            r1, r2 = t1 * cos - t2 * sin, t1 * sin + t2 * cos
            if part == 1:
                r1, r2 = r1 * (dk ** -0.5), r2 * (dk ** -0.5)
            o_ref[:, off:off + half] = r1
            o_ref[:, off + half:off + dk] = r2
    o_ref[:, 2 * hk:2 * hk + hv] = y[:, 2 * hk:2 * hk + hv]
    gt = y[:, 2 * hk + hv:]
    o_ref[:, 2 * hk + hv:] = gt * jax.nn.sigmoid(gt)


def _ret_proj(x, g, w, pos, seq, *, tm=256):
    n, d = x.shape
    nout = w.shape[1]
    nh = RET_HEADS
    dk = nout // 6 // nh
    dv = 2 * dk
    half = dk // 2
    tm = _row_tile(n, tm)
    inv = ROPE_BASE ** (-jnp.arange(half, dtype=F32) / half)
    ang = pos.astype(F32)[:, None] * inv[None, :]
    cos, sin = jnp.cos(ang), jnp.sin(ang)
    if seq >= tm:
        assert seq % tm == 0
        per_seq = seq // tm
        table_map = lambda i: (i % per_seq, 0)
    else:
        assert tm % seq == 0
        cos, sin = jnp.tile(cos, (tm // seq, 1)), jnp.tile(sin, (tm // seq, 1))
        table_map = lambda i: (0, 0)
    return pl.pallas_call(
        functools.partial(_ret_proj_body, nh=nh, dk=dk, dv=dv),
        grid=(n // tm,),
        in_specs=[pl.BlockSpec((tm, d), lambda i: (i, 0)), _resident((1, d)), _resident(w.shape),
                  pl.BlockSpec((tm, half), table_map), pl.BlockSpec((tm, half), table_map)],
        out_specs=pl.BlockSpec((tm, nout), lambda i: (i, 0)),
        out_shape=jax.ShapeDtypeStruct((n, nout), F32),
        compiler_params=_params("parallel"),
        name="ret_proj",
    )(x, g.reshape(1, d), w, cos, sin)


def _proj_res_body(x_ref, w_ref, r_ref, o_ref):
    o_ref[...] = r_ref[...] + jnp.dot(x_ref[...], w_ref[...], preferred_element_type=F32)


def _proj_res(x, w, res, *, tm=512):
    n, k = x.shape
    d = w.shape[1]
    tm = _row_tile(n, tm)
    return pl.pallas_call(
        _proj_res_body,
        grid=(n // tm,),
        in_specs=[pl.BlockSpec((tm, k), lambda i: (i, 0)), _resident(w.shape),
                  pl.BlockSpec((tm, d), lambda i: (i, 0))],
        out_specs=pl.BlockSpec((tm, d), lambda i: (i, 0)),
        out_shape=jax.ShapeDtypeStruct((n, d), F32),
        compiler_params=_params("parallel"),
        name="proj_res",
    )(x, w, res)


def _ret_body(*refs, nh, dk, dv, c, cp, has_s0):
    if has_s0:
        (q_ref, k_ref, v_ref, g_ref, din_ref, dq_ref, dkk_ref, dc_ref, gain_ref, s0_ref, o_ref, s_ref) = refs
    else:
        (q_ref, k_ref, v_ref, g_ref, din_ref, dq_ref, dkk_ref, dc_ref, gain_ref, o_ref, s_ref) = refs
    i = pl.program_id(1)

    @pl.when(i == 0)
    def _():
        if has_s0:
            s_ref[...] = s0_ref[...]
        else:
            s_ref[...] = jnp.zeros(s_ref.shape, F32)

    def pad(t):
        if cp == c:
            return t
        return jnp.concatenate([t, jnp.zeros((cp - c, t.shape[1]), t.dtype)], axis=0)

    heads = range(nh)
    nt = (((1,), (1,)), ((), ()))
    kr = [pad(k_ref[0, :, h * dk:(h + 1) * dk]) for h in heads]
    qb = [pad(q_ref[0, :, h * dk:(h + 1) * dk]).astype(BF16) for h in heads]
    vb = [pad(v_ref[0, :, h * dv:(h + 1) * dv]).astype(BF16) for h in heads]
    att = [lax.dot_general(qb[h], kr[h].astype(BF16), nt, preferred_element_type=F32) for h in heads]
    o_x = [jnp.dot(qb[h], s_ref[0, h].astype(BF16), preferred_element_type=F32) for h in heads]
    o_in = [jnp.dot((att[h] * din_ref[h]).astype(BF16), vb[h], preferred_element_type=F32) for h in heads]
    upd = [jnp.dot((kr[h] * dkk_ref[h]).T.astype(BF16), vb[h], preferred_element_type=F32) for h in heads]
    for h in heads:
        s_ref[0, h] = s_ref[0, h] * dc_ref[h] + upd[h]
    for h in heads:
        o = (o_in[h] + o_x[h] * dq_ref[h])[:c]
        mu = jnp.mean(o, axis=-1, keepdims=True)
        d = o - mu
        var = jnp.mean(d * d, axis=-1, keepdims=True)
        on = d * lax.rsqrt(var + EPS) * gain_ref[:, h * dv:(h + 1) * dv]
        o_ref[0, :, h * dv:(h + 1) * dv] = (g_ref[0, :, h * dv:(h + 1) * dv] * on).astype(BF16)


def _retention(proj, gn_gain, s0):
    b, l, width = proj.shape
    nh = RET_HEADS
    hk = width // 6
    hv = 2 * hk
    dk, dv = hk // nh, hv // nh
    c = math.gcd(l, RET_CHUNK)
    cp = RET_CHUNK
    n = l // c
    log_g = jnp.log1p(-jnp.exp2(-5.0 - jnp.arange(nh, dtype=F32)))
    idx = jnp.arange(c, dtype=F32)
    rel = idx[:, None] - idx[None, :]
    d_in = jnp.where(rel[None] >= 0, jnp.exp(log_g[:, None, None] * jnp.maximum(rel, 0.0)[None]), 0.0)
    d_q = jnp.exp(log_g[:, None] * (idx[None, :] + 1.0))
    d_k = jnp.exp(log_g[:, None] * (c - 1.0 - idx[None, :]))
    d_c = jnp.exp(log_g * c)
    d_in = jnp.pad(d_in, ((0, 0), (0, cp - c), (0, cp - c)))
    d_q = jnp.pad(d_q, ((0, 0), (0, cp - c)))[:, :, None]
    d_k = jnp.pad(d_k, ((0, 0), (0, cp - c)))[:, :, None]
    d_c = d_c[:, None, None]

    has_s0 = s0 is not None
    ins = [proj, proj, proj, proj, d_in, d_q, d_k, d_c, gn_gain.reshape(1, hv)]
    specs = [
        pl.BlockSpec((1, c, hk), lambda bi, i: (bi, i, 0)),
        pl.BlockSpec((1, c, hk), lambda bi, i: (bi, i, 1)),
        pl.BlockSpec((1, c, hv), lambda bi, i: (bi, i, 1)),
        pl.BlockSpec((1, c, hv), lambda bi, i: (bi, i, 2)),
        _resident(d_in.shape), _resident(d_q.shape), _resident(d_k.shape), _resident(d_c.shape),
        _resident((1, hv)),
    ]
    if has_s0:
        ins.append(s0)
        specs.append(pl.BlockSpec((1, nh, dk, dv), lambda bi, i: (bi, 0, 0, 0)))
    return pl.pallas_call(
        functools.partial(_ret_body, nh=nh, dk=dk, dv=dv, c=c, cp=cp, has_s0=has_s0),
        grid=(b, n),
        in_specs=specs,
        out_specs=[pl.BlockSpec((1, c, hv), lambda bi, i: (bi, i, 0)),
                   pl.BlockSpec((1, nh, dk, dv), lambda bi, i: (bi, 0, 0, 0))],
        out_shape=[jax.ShapeDtypeStruct((b, l, hv), BF16),
                   jax.ShapeDtypeStruct((b, nh, dk, dv), F32)],
        compiler_params=_params("parallel", "arbitrary"),
        name="retention",
    )(*ins)


def _qkv_body(x_ref, g_ref, w_ref, q_ref, k_ref, v_ref, kb_ref, vt_ref, km_ref, *, d, blk, scale):
    h = _rms(x_ref[...], g_ref[...]).astype(BF16)
    y = jnp.dot(h, w_ref[...], preferred_element_type=F32)
    k = y[:, d:2 * d]
    v = y[:, 2 * d:]
    q_ref[...] = (y[:, :d] * scale).astype(BF16)
    k_ref[...] = k.reshape(k_ref.shape)
    v_ref[...] = v.reshape(v_ref.shape)
    kb_ref[...] = k.astype(BF16)
    vt_ref[0] = v.T.astype(BF16)
    for r in range(k.shape[0] // blk):
        km_ref[r] = jnp.sum(k[r * blk:(r + 1) * blk], axis=0, keepdims=True) * (1.0 / blk)


def _moba_qkv(x, g, w, seq, *, tm=256):
    n, d = x.shape
    hd = d // MOBA_HEADS
    blk = MOBA_BLOCK
    tm = _row_tile(n, tm)
    assert tm % blk == 0 and seq % tm == 0
    per_seq = seq // tm
    row = lambda i: (i, 0)
    return pl.pallas_call(
        functools.partial(_qkv_body, d=d, blk=blk, scale=LOG2E * hd ** -0.5),
        grid=(n // tm,),
        in_specs=[pl.BlockSpec((tm, d), row), _resident((1, d)), _resident(w.shape)],
        out_specs=[pl.BlockSpec((tm, d), row),
                   pl.BlockSpec((tm, MOBA_HEADS, hd), lambda i: (i, 0, 0)),
                   pl.BlockSpec((tm, MOBA_HEADS, hd), lambda i: (i, 0, 0)),
                   pl.BlockSpec((tm, d), row),
                   pl.BlockSpec((1, d, tm), lambda i: (i // per_seq, 0, i % per_seq)),
                   pl.BlockSpec((tm // blk, 1, d), lambda i: (i, 0, 0))],
        out_shape=[jax.ShapeDtypeStruct((n, d), BF16), jax.ShapeDtypeStruct((n, MOBA_HEADS, hd), F32),
                   jax.ShapeDtypeStruct((n, MOBA_HEADS, hd), F32), jax.ShapeDtypeStruct((n, d), BF16),
                   jax.ShapeDtypeStruct((n // seq, d, seq), BF16),
                   jax.ShapeDtypeStruct((n // blk, 1, d), F32)],
        compiler_params=_params("parallel"),
        name="moba_qkv",
    )(x, g.reshape(1, d), w)


def _qkv_small_body(x_ref, g_ref, w_ref, q_ref, k_ref, v_ref, *, d, scale):
    h = _rms(x_ref[...], g_ref[...]).astype(BF16)
    y = jnp.dot(h, w_ref[...], preferred_element_type=F32)
    q_ref[...] = y[:, :d] * scale
    k_ref[...] = y[:, d:2 * d]
    v_ref[...] = y[:, 2 * d:]


def _moba_qkv_small(x, g, w):
    n, d = x.shape
    hd = d // MOBA_HEADS
    full = pl.BlockSpec((n, d), lambda i: (0, 0))
    return pl.pallas_call(
        functools.partial(_qkv_small_body, d=d, scale=hd ** -0.5),
        grid=(1,),
        in_specs=[full, _resident((1, d)), _resident(w.shape)],
        out_specs=[full, full, full],
        out_shape=[jax.ShapeDtypeStruct((n, d), F32)] * 3,
        compiler_params=_params("arbitrary"),
        name="moba_qkv_small",
    )(x, g.reshape(1, d), w)


def _t5_bucket_np(dist):
    n = np.maximum(dist, 0)
    max_exact = REL_BUCKETS // 2
    nf = np.maximum(n, 1).astype(np.float64)
    large = max_exact + (np.log(nf / max_exact) / math.log(REL_MAX_DIST / max_exact)
                         * (REL_BUCKETS - max_exact)).astype(np.int64)
    large = np.minimum(large, REL_BUCKETS - 1)
    return np.where(n < max_exact, n, large).astype(np.int32)


def _bias_body(bk_ref, tab_ref, o_ref, *, scale):
    bk = bk_ref[0]
    acc = jnp.where(bk < 0, NEG_INF, 0.0).astype(F32)
    for b in range(REL_BUCKETS):
        acc = jnp.where(bk == b, tab_ref[0, b] * scale, acc)
    o_ref[0, 0] = acc


def _bias_tiles(buckets, table, scale=1.0):
    t, r, w = buckets.shape
    g, nbuck, tr, tw = table.shape
    assert nbuck == REL_BUCKETS and tr in (1, r) and tw in (1, w)
    return pl.pallas_call(
        functools.partial(_bias_body, scale=scale),
        grid=(g, t),
        in_specs=[pl.BlockSpec((1, r, w), lambda gi, ti: (ti, 0, 0)),
                  pl.BlockSpec((1, REL_BUCKETS, tr, tw), lambda gi, ti: (gi, 0, 0, 0))],
        out_specs=pl.BlockSpec((1, 1, r, w), lambda gi, ti: (gi, ti, 0, 0)),
        out_shape=jax.ShapeDtypeStruct((g, t, r, w), F32),
        compiler_params=_params("parallel", "parallel"),
        name="t5_bias_tiles",
    )(jnp.asarray(buckets), table)


FLASH_HEADS = 8
DEN_ROWS = 16


def _flash_body(q_ref, k_ref, vt_ref, km_ref, bias_ref, o_ref, *, nb, blk, hd, topk, nch):
    pair = 2 * hd
    lane = lax.broadcasted_iota(jnp.int32, (1, pair), 1)
    row = lax.broadcasted_iota(jnp.int32, (pair, 1), 0)
    bidx = lax.broadcasted_iota(jnp.int32, (nb, 1), 0)
    nt = (((1,), (1,)), ((), ()))

    wq = 2 * blk

    def lanes(pc):
        return slice(pc * pair, (pc + 1) * pair)

    def both(pc, idx):
        return jnp.concatenate([bias_ref[(2 * pc,) + idx], bias_ref[(2 * pc + 1,) + idx]], axis=1)

    def masked_q(pc):
        qp = q_ref[0, :, lanes(pc)]
        zero = jnp.zeros((blk, pair), BF16)
        return jnp.concatenate([jnp.where(lane < hd, qp, zero), jnp.where(lane >= hd, qp, zero)], axis=0)

    def q_block(t):
        npair = nch // 2
        qs = [masked_q(pc) for pc in range(npair)]
        adds = []
        for pc in range(npair):
            add = [None] * (t + 1)
            add[t] = both(pc, (0,))
            if t >= 1:
                gate = lax.dot_general(km_ref[0, :, lanes(pc)].astype(BF16), qs[pc], nt,
                                       preferred_element_type=F32)
                valid = bidx < t
                gate = jnp.where(valid, gate, NEG_INF)
                rank = jnp.zeros((nb, wq), jnp.int32)
                for j2 in range(t):
                    r2 = gate[j2:j2 + 1, :]
                    ahead = (r2 > gate) | ((r2 == gate) & (j2 < bidx))
                    rank = rank + ahead.astype(jnp.int32)
                sel = valid & (rank < topk)
                add[t - 1] = both(pc, (1,)) + jnp.where(sel[t - 1:t], 0.0, NEG_INF)
                far = jnp.where(sel, both(pc, (2, slice(0, 1))), NEG_INF)
                for j in range(t - 1):
                    add[j] = far[j:j + 1]
            adds.append(add)

        order = [t] + list(range(t))

        def qk(pc, j):
            return lax.dot_general(k_ref[0, j * blk:(j + 1) * blk, lanes(pc)], qs[pc], nt,
                                   preferred_element_type=F32)

        ones = jnp.ones((DEN_ROWS, blk), BF16)

        def pv(pc, j, p):
            vt1 = jnp.concatenate([vt_ref[0, lanes(pc), j * blk:(j + 1) * blk], ones], axis=0)
            return jnp.dot(vt1, p, preferred_element_type=F32)

        state = [dict(x={}, m=None, acc=None) for _ in range(npair)]

        def pass1(pc):
            st = state[pc]
            pend = None
            for j in order:
                s = qk(pc, j)
                yield
                if pend is not None:
                    finish(st, pc, *pend)
                pend = (j, s)
            finish(st, pc, *pend)
            yield

        def finish(st, pc, j, s):
            add = adds[pc][j]
            if add.shape[0] == 1:
                st["x"][j] = s
                mx = s.max(axis=0, keepdims=True) + add
            else:
                st["x"][j] = s + add
                mx = st["x"][j].max(axis=0, keepdims=True)
            st["m"] = mx if st["m"] is None else jnp.maximum(st["m"], mx)

        def pass2(pc):
            st = state[pc]
            pend = None
            for j in order:
                add = adds[pc][j]
                shift = st["m"] - add if add.shape[0] == 1 else st["m"]
                p = jnp.exp2(st["x"][j] - shift).astype(BF16)
                yield
                if pend is not None:
                    a = pv(pc, *pend)
                    st["acc"] = a if st["acc"] is None else st["acc"] + a
                pend = (j, p)
            a = pv(pc, *pend)
            st["acc"] = a if st["acc"] is None else st["acc"] + a
            yield

        def run_together(*gens):
            gens = list(gens)
            while gens:
                for g in list(gens):
                    try:
                        next(g)
                    except StopIteration:
                        gens.remove(g)

        run_together(pass1(0))
        for pc in range(npair):
            if pc + 1 < npair:
                run_together(pass1(pc + 1), pass2(pc))
            else:
                run_together(pass2(pc))
        for pc in range(npair):
            st = state[pc]
            acc = st["acc"][:pair] / st["acc"][pair:pair + 1]
            out_t = jnp.where(row >= hd, acc[:, blk:], acc[:, :blk])
            o_ref[0, :, lanes(pc)] = out_t.T.astype(BF16)

    for t in range(nb):
        pl.when(pl.program_id(2) == t)(functools.partial(q_block, t))


def _moba_flash(q, kb, vt, kmean, bias):
    b, l, d = q.shape
    hd = d // MOBA_HEADS
    pair = 2 * hd
    assert pair == LANES
    blk = MOBA_BLOCK
    nb = l // blk
    nch = FLASH_HEADS
    w = nch * hd
    assert nch % 2 == 0 and d % w == 0
    return pl.pallas_call(
        functools.partial(_flash_body, nb=nb, blk=blk, hd=hd, topk=MOBA_TOPK, nch=nch),
        grid=(b, d // w, nb),
        in_specs=[pl.BlockSpec((1, blk, w), lambda bi, p, qi: (bi, qi, p)),
                  pl.BlockSpec((1, l, w), lambda bi, p, qi: (bi, 0, p)),
                  pl.BlockSpec((1, w, l), lambda bi, p, qi: (bi, p, 0)),
                  pl.BlockSpec((1, nb, w), lambda bi, p, qi: (bi, 0, p)),
                  _resident(bias.shape) if d == w else
                  pl.BlockSpec((nch, 3, blk, blk), lambda bi, p, qi: (p, 0, 0, 0))],
        out_specs=pl.BlockSpec((1, blk, w), lambda bi, p, qi: (bi, qi, p)),
        out_shape=jax.ShapeDtypeStruct((b, l, d), BF16),
        compiler_params=_params("parallel", "parallel", "arbitrary"),
        name="moba_flash",
    )(q, kb, vt, kmean, bias)


def _prompt_bias_buckets(blk):
    assert REL_MAX_DIST <= blk + 1
    key = np.arange(blk)[:, None]
    qry = np.arange(blk)[None, :]
    tiles = []
    for dblk in range(3):
        dist = dblk * blk + qry - key
        tiles.append(np.where(dist >= 0, _t5_bucket_np(dist), -1))
    return np.stack(tiles).astype(np.int32)


SAMPLE_BLOCKS_PER_STEP = 8


def _sample_body(pt_ref, bsel_ref, qbdt_ref, *refs, nbp, bps, heads, lq, topk):
    npg = 2 * bps
    k_refs, v_refs = refs[:npg], refs[npg:2 * npg]
    kn_ref, vn_ref, bias_ref, bias_own_ref, o_ref, m_ref, l_ref, g_ref, acc_ref = refs[2 * npg:]
    step = pl.program_id(1)
    hq = heads * lq
    qbdt = qbdt_ref[0]
    d = qbdt.shape[1]
    hd = d // heads
    lane = lax.broadcasted_iota(jnp.int32, (1, LANES), 1)
    nt = (((1,), (1,)), ((), ()))

    @pl.when(step == 0)
    def _():
        m_ref[...] = jnp.zeros(m_ref.shape, F32)
        l_ref[...] = jnp.zeros(l_ref.shape, F32)
        g_ref[...] = jnp.zeros(g_ref.shape, F32)

    def block(k, v, bias, token_major=False):
        kb, vb = k.astype(BF16), v.astype(BF16)
        if token_major:
            sr = lax.dot_general(qbdt, kb, nt, preferred_element_type=F32)
        else:
            sr = jnp.dot(qbdt, kb, preferred_element_type=F32)
        g = jnp.sum(sr, axis=1, keepdims=True) * (1.0 / sr.shape[1])
        sc = sr + bias
        m = jnp.max(sc, axis=1, keepdims=True)
        p = jnp.exp(sc - m)
        l = jnp.sum(p, axis=1, keepdims=True)
        if token_major:
            acc = jnp.dot(p.astype(BF16), vb, preferred_element_type=F32)
        else:
            acc = lax.dot_general(p.astype(BF16), vb, nt, preferred_element_type=F32)
        slab = jnp.concatenate(
            [acc[h * lq:(h + 1) * lq, (h * hd // LANES) * LANES:(h * hd // LANES + 1) * LANES]
             for h in range(heads)], axis=0)
        return m, l, g, slab

    stats = []
    for jj in range(bps):
        j = step * bps + jj
        kt = jnp.concatenate([k_refs[2 * jj][0], k_refs[2 * jj + 1][0]], axis=1)
        vt = jnp.concatenate([v_refs[2 * jj][0], v_refs[2 * jj + 1][0]], axis=1)
        m, l, g, slab = block(kt, vt, bias_ref[bsel_ref[j]])
        acc_ref[j] = slab
        stats.append((j, m, l, g))
    m_all, l_all, g_all = m_ref[...], l_ref[...], g_ref[...]
    for j, m, l, g in stats:
        hit = lane == j
        m_all = jnp.where(hit, m, m_all)
        l_all = jnp.where(hit, l, l_all)
        g_all = jnp.where(hit, g, g_all)
    m_ref[...] = m_all
    l_ref[...] = l_all
    g_ref[...] = g_all

    @pl.when(step == nbp // bps - 1)
    def _():
        zpad = jnp.zeros((PAGE_SIZE - lq, d), F32)
        m, l, _, slab = block(jnp.concatenate([kn_ref[0], zpad], axis=0),
                              jnp.concatenate([vn_ref[0], zpad], axis=0), bias_own_ref[...], token_major=True)
        acc_ref[nbp] = slab
        own = lane == nbp
        m_fin = jnp.where(own, m, m_all)
        l_fin = jnp.where(own, l, l_all)
        lane_f = lane.astype(F32)
        gate = jnp.where(lane < nbp, g_all, NEG_INF)
        sel = jnp.broadcast_to(own, (hq, LANES))
        for _ in range(min(topk, nbp)):
            best = jnp.max(gate, axis=1, keepdims=True)
            first = jnp.min(jnp.where(gate == best, lane_f, float(LANES)), axis=1, keepdims=True)
            pick = lane_f == first
            sel = sel | pick
            gate = jnp.where(pick, NEG_INF, gate)
        top = jnp.max(jnp.where(sel, m_fin, NEG_INF), axis=1, keepdims=True)
        w = jnp.where(sel, jnp.exp(m_fin - top), 0.0)
        wn = w / jnp.sum(w * l_fin, axis=1, keepdims=True)
        out = jnp.zeros((hq, LANES), F32)
        for jj in range(nbp + 1):
            out = out + wn[:, jj:jj + 1] * acc_ref[jj]
        rrow = lax.broadcasted_iota(jnp.int32, (hq, 1), 0)
        keep = (lane // hd) == ((rrow // lq) % (LANES // hd))
        o_ref[0] = jnp.where(keep, out, 0.0)


def _moba_sample(q, k_new, v_new, cache_kt, cache_vt, page_base, page_table, rel_bias):
    b, lq, d = q.shape
    heads = MOBA_HEADS
    hd = d // heads
    hq = heads * lq
    blk = MOBA_BLOCK
    n_pages = page_table.shape[1]
    past = n_pages * PAGE_SIZE
    assert hq == LANES and blk == 2 * PAGE_SIZE and past % blk == 0 and lq <= PAGE_SIZE
    nbp = past // blk
    bps = math.gcd(nbp, SAMPLE_BLOCKS_PER_STEP)
    assert nbp + 1 <= LANES

    same_head = (np.arange(hq)[:, None] // lq) == (np.arange(d)[None, :] // hd)
    qbdt = jnp.where(jnp.asarray(same_head), jnp.tile(q, (1, heads, 1)), 0.0).astype(BF16)

    qry = np.tile(np.arange(lq), heads)[:, None]
    key = np.arange(blk)[None, :]
    tiles, bsel = [], []
    for jb in range(nbp):
        dist = past + qry - (jb * blk + key)
        bk = np.where(dist >= 0, _t5_bucket_np(dist), -1).astype(np.int32)
        for t, existing in enumerate(tiles):
            if np.array_equal(existing, bk):
                bsel.append(t)
                break
        else:
            tiles.append(bk)
            bsel.append(len(tiles) - 1)
    table = jnp.repeat(rel_bias, lq, axis=1)[None, :, :, None]
    bias = _bias_tiles(np.stack(tiles), table)[0]
    dist_own = qry - np.arange(PAGE_SIZE)[None, :]
    own_bk = np.where(dist_own >= 0, _t5_bucket_np(dist_own), -1).astype(np.int32)
    bias_own = _bias_tiles(own_bk[None], table)[0, 0]
    nt = bias.shape[0]

    npg = 2 * bps

    def page(t):
        return lambda bi, s, pt, bs: (page_base + pt[bi * n_pages + s * npg + t], 0, 0)

    per_b = lambda bi, s, pt, bs: (bi, 0, 0)
    page_specs = [pl.BlockSpec((1, d, PAGE_SIZE), page(t)) for t in range(npg)]
    grid_spec = pltpu.PrefetchScalarGridSpec(
        num_scalar_prefetch=2,
        grid=(b, nbp // bps),
        in_specs=[pl.BlockSpec((1, hq, d), per_b)] + page_specs + page_specs + [
            pl.BlockSpec((1, lq, d), per_b), pl.BlockSpec((1, lq, d), per_b),
            pl.BlockSpec((nt, hq, blk), lambda bi, s, pt, bs: (0, 0, 0)),
            pl.BlockSpec((hq, PAGE_SIZE), lambda bi, s, pt, bs: (0, 0))],
        out_specs=pl.BlockSpec((1, hq, LANES), per_b),
        scratch_shapes=[pltpu.VMEM((hq, LANES), F32), pltpu.VMEM((hq, LANES), F32),
                        pltpu.VMEM((hq, LANES), F32), pltpu.VMEM((nbp + 1, hq, LANES), F32)],
    )
    o2 = pl.pallas_call(
        functools.partial(_sample_body, nbp=nbp, bps=bps, heads=heads, lq=lq, topk=MOBA_TOPK),
        grid_spec=grid_spec,
        out_shape=jax.ShapeDtypeStruct((b, hq, LANES), F32),
        compiler_params=_params("parallel", "arbitrary"),
        name="moba_sample",
    )(page_table.reshape(-1), jnp.asarray(np.array(bsel, np.int32)), qbdt,
      *([cache_kt] * npg), *([cache_vt] * npg), k_new, v_new, bias, bias_own)
    o = o2.reshape(b, heads, lq, LANES // hd, hd).sum(axis=3)
    return o.transpose(0, 2, 1, 3).reshape(b, lq, d).astype(BF16)


def kernel(x_prompt, x_sample, state_ret, cache_k, cache_v, page_table, ffn1_norm, ffn1_w_in, ffn1_w_out,
           mix_norm, ffn2_norm, ffn2_w_in, ffn2_w_out, ret_w_in, ret_gn_gain, ret_w_out, moba_w_qkv,
           moba_w_out, rel_bias, final_norm):
    bp, lp, d = x_prompt.shape
    bs, ls, _ = x_sample.shape
    depth = ffn1_norm.shape[0]
    past = page_table.shape[1] * PAGE_SIZE
    pos_p = jnp.arange(lp, dtype=jnp.int32)
    pos_s = past + jnp.arange(ls, dtype=jnp.int32)
    hp = x_prompt.reshape(bp * lp, d)
    hs = x_sample.reshape(bs * ls, d)
    cast = lambda w: w.astype(BF16)
    w1i, w1o, w2i, w2o = cast(ffn1_w_in), cast(ffn1_w_out), cast(ffn2_w_in), cast(ffn2_w_out)
    ret_p, ret_s, kp_l, vp_l, ks_l, vs_l = [], [], [], [], [], []
    for i in range(depth):
        hp = _ffn(hp, ffn1_norm[i], w1i, w1o, i)
        hs = _ffn(hs, ffn1_norm[i], w1i, w1o, i)
        if i % 2 == 0:
            r = i // 2
            w_in, w_out = cast(ret_w_in[r]), cast(ret_w_out[r])
            proj_p = _ret_proj(hp, mix_norm[i], w_in, pos_p, lp).reshape(bp, lp, -1)
            proj_s = _ret_proj(hs, mix_norm[i], w_in, pos_s, ls).reshape(bs, ls, -1)
            gp, sp = _retention(proj_p, ret_gn_gain[r], None)
            gs, ss = _retention(proj_s, ret_gn_gain[r], state_ret[r])
            hp = _proj_res(gp.reshape(bp * lp, -1), w_out, hp)
            hs = _proj_res(gs.reshape(bs * ls, -1), w_out, hs)
            ret_p.append(sp)
            ret_s.append(ss)
        else:
            m = i // 2
            w_qkv, w_out = cast(moba_w_qkv[m]), cast(moba_w_out[m])
            hd = d // MOBA_HEADS
            q, k, v, kb, vt, km = _moba_qkv(hp, mix_norm[i], w_qkv, lp)
            bias = _bias_tiles(_prompt_bias_buckets(MOBA_BLOCK), rel_bias.T[:, :, None, None], LOG2E)
            op = _moba_flash(q.reshape(bp, lp, d), kb.reshape(bp, lp, d), vt,
                             km.reshape(bp, lp // MOBA_BLOCK, d), bias)
            hp = _proj_res(op.reshape(bp * lp, d), w_out, hp)
            kp_l.append(k.reshape(bp, lp, MOBA_HEADS, hd))
            vp_l.append(v.reshape(bp, lp, MOBA_HEADS, hd))

            qs, ksn, vsn = _moba_qkv_small(hs, mix_norm[i], w_qkv)
            n_pool = cache_k.shape[1]
            pages_t = lambda cache: cache.transpose(0, 1, 3, 4, 2).reshape(-1, d, PAGE_SIZE)
            os_ = _moba_sample(qs.reshape(bs, ls, d), ksn.reshape(bs, ls, d), vsn.reshape(bs, ls, d),
                               pages_t(cache_k), pages_t(cache_v), m * n_pool, page_table, rel_bias)
            hs = _proj_res(os_.reshape(bs * ls, d), w_out, hs)
            ks_l.append(ksn.reshape(bs, ls, MOBA_HEADS, hd))
            vs_l.append(vsn.reshape(bs, ls, MOBA_HEADS, hd))
        last = i == depth - 1
        hp = _ffn(hp, ffn2_norm[i], w2i, w2o, i, final_norm if last else None)
        hs = _ffn(hs, ffn2_norm[i], w2i, w2o, i, final_norm if last else None)
    return (hp.reshape(bp, lp, d), hs.reshape(bs, ls, d), jnp.stack(ret_p), jnp.stack(ret_s),
            jnp.stack(kp_l), jnp.stack(vp_l), jnp.stack(ks_l), jnp.stack(vs_l))
```

```python
import functools
import math

import jax
import jax.numpy as jnp
import numpy as np
from jax import lax
from jax.experimental import pallas as pl
from jax.experimental.pallas import tpu as pltpu

F32 = jnp.float32
BF16 = jnp.bfloat16
NEG_INF = float("-inf")
LOG2E = math.log2(math.e)

LANES = 128
SUBLANES = 8
VMEM_LIMIT_BYTES = 56 * 1024 * 1024

EPS = 1e-6
ROPE_BASE = 10000.0
RET_HEADS = 4
RET_CHUNK = 128
MOBA_HEADS = 16
MOBA_BLOCK = 256
MOBA_TOPK = 3
PAGE_SIZE = 128
REL_BUCKETS = 32
REL_MAX_DIST = 128


def _params(*semantics):
    return pltpu.CompilerParams(dimension_semantics=semantics, vmem_limit_bytes=VMEM_LIMIT_BYTES)


def _resident(shape):
    zeros = (0,) * len(shape)
    return pl.BlockSpec(shape, lambda *_: zeros, pipeline_mode=pl.Buffered(1))


def _resident_layer(stacked, layer):
    idx = (layer,) + (0,) * (stacked.ndim - 1)
    return pl.BlockSpec((None,) + stacked.shape[1:], lambda *_: idx, pipeline_mode=pl.Buffered(1))


def _row_tile(n, want):
    t = min(n, want)
    assert n % t == 0, (n, t)
    return t


def _rms(x, g):
    y = x * lax.rsqrt(jnp.mean(x * x, axis=-1, keepdims=True) + EPS)
    return y * g


def _ffn_body(*refs, d_ff, tf, final):
    if final:
        x_ref, g_ref, win_ref, wout_ref, fg_ref, o_ref = refs
    else:
        x_ref, g_ref, win_ref, wout_ref, o_ref = refs
    x = x_ref[...]
    h = _rms(x, g_ref[...]).astype(BF16)
    y = None
    for c in range(d_ff // tf):
        a = jnp.dot(h, win_ref[:, c * tf:(c + 1) * tf], preferred_element_type=F32)
        b = jnp.dot(h, win_ref[:, d_ff + c * tf:d_ff + (c + 1) * tf], preferred_element_type=F32)
        gated = (a * jax.nn.sigmoid(a) * b).astype(BF16)
        part = jnp.dot(gated, wout_ref[c * tf:(c + 1) * tf, :], preferred_element_type=F32)
        y = part if y is None else y + part
    out = x + 0.5 * y
    if final:
        out = _rms(out, fg_ref[...])
    o_ref[...] = out


MXU_TILE = 256


def _ffn(x, g, w_in, w_out, layer, final_g=None, *, tm=512):
    n, d = x.shape
    d_ff = w_out.shape[1]
    tm = _row_tile(n, tm)
    tf = d_ff if d_ff % MXU_TILE == 0 else LANES
    assert d_ff % tf == 0
    final = final_g is not None
    ins = [x, g.reshape(1, d), w_in, w_out]
    specs = [pl.BlockSpec((tm, d), lambda i: (i, 0)), _resident((1, d)),
             _resident_layer(w_in, layer), _resident_layer(w_out, layer)]
    if final:
        ins.append(final_g.reshape(1, d))
        specs.append(_resident((1, d)))
    return pl.pallas_call(
        functools.partial(_ffn_body, d_ff=d_ff, tf=tf, final=final),
        grid=(n // tm,),
        in_specs=specs,
        out_specs=pl.BlockSpec((tm, d), lambda i: (i, 0)),
        out_shape=jax.ShapeDtypeStruct((n, d), F32),
        compiler_params=_params("parallel"),
        name="ffn",
    )(*ins)


def _ret_proj_body(x_ref, g_ref, w_ref, cos_ref, sin_ref, o_ref, *, nh, dk, dv):
    h = _rms(x_ref[...], g_ref[...]).astype(BF16)
    y = jnp.dot(h, w_ref[...], preferred_element_type=F32)
    cos, sin = cos_ref[...], sin_ref[...]
    half, hk, hv = dk // 2, nh * dk, nh * dv
    for part in range(2):
        for hh in range(nh):
            off = part * hk + hh * dk
            t1, t2 = y[:, off:off + half], y[:, off + half:off + dk]
            r1, r2 = t1 * cos - t2 * sin, t1 * sin + t2 * cos
            if part == 1:
                r1, r2 = r1 * (dk ** -0.5), r2 * (dk ** -0.5)
            o_ref[:, off:off + half] = r1
            o_ref[:, off + half:off + dk] = r2
    o_ref[:, 2 * hk:2 * hk + hv] = y[:, 2 * hk:2 * hk + hv]
    gt = y[:, 2 * hk + hv:]
    o_ref[:, 2 * hk + hv:] = gt * jax.nn.sigmoid(gt)


def _ret_proj(x, g, w, pos, seq, *, tm=256):
    n, d = x.shape
    nout = w.shape[1]
    nh = RET_HEADS
    dk = nout // 6 // nh
    dv = 2 * dk
    half = dk // 2
    tm = _row_tile(n, tm)
    cos, sin = _rope_tables(pos, half)
    if seq >= tm:
        assert seq % tm == 0
        per_seq = seq // tm
        table_map = lambda i: (i % per_seq, 0)
    else:
        assert tm % seq == 0
        cos, sin = jnp.tile(cos, (tm // seq, 1)), jnp.tile(sin, (tm // seq, 1))
        table_map = lambda i: (0, 0)
    return pl.pallas_call(
        functools.partial(_ret_proj_body, nh=nh, dk=dk, dv=dv),
        grid=(n // tm,),
        in_specs=[pl.BlockSpec((tm, d), lambda i: (i, 0)), _resident((1, d)), _resident(w.shape),
                  pl.BlockSpec((tm, half), table_map), pl.BlockSpec((tm, half), table_map)],
        out_specs=pl.BlockSpec((tm, nout), lambda i: (i, 0)),
        out_shape=jax.ShapeDtypeStruct((n, nout), F32),
        compiler_params=_params("parallel"),
        name="ret_proj",
    )(x, g.reshape(1, d), w, cos, sin)


def _proj_res_body(x_ref, w_ref, r_ref, o_ref):
    o_ref[...] = r_ref[...] + jnp.dot(x_ref[...], w_ref[...], preferred_element_type=F32)


def _proj_res(x, w, res, *, tm=512):
    n, k = x.shape
    d = w.shape[1]
    tm = _row_tile(n, tm)
    return pl.pallas_call(
        _proj_res_body,
        grid=(n // tm,),
        in_specs=[pl.BlockSpec((tm, k), lambda i: (i, 0)), _resident(w.shape),
                  pl.BlockSpec((tm, d), lambda i: (i, 0))],
        out_specs=pl.BlockSpec((tm, d), lambda i: (i, 0)),
        out_shape=jax.ShapeDtypeStruct((n, d), F32),
        compiler_params=_params("parallel"),
        name="proj_res",
    )(x, w, res)


def _ret_body(*refs, nh, dk, dv, c, cp, has_s0):
    if has_s0:
        (q_ref, k_ref, v_ref, g_ref, din_ref, dq_ref, dkk_ref, dc_ref, gain_ref, s0_ref, o_ref, s_ref) = refs
    else:
        (q_ref, k_ref, v_ref, g_ref, din_ref, dq_ref, dkk_ref, dc_ref, gain_ref, o_ref, s_ref) = refs
    i = pl.program_id(1)

    @pl.when(i == 0)
    def _():
        if has_s0:
            s_ref[...] = s0_ref[...]
        else:
            s_ref[...] = jnp.zeros(s_ref.shape, F32)

    def pad(t):
        if cp == c:
            return t
        return jnp.concatenate([t, jnp.zeros((cp - c, t.shape[1]), t.dtype)], axis=0)

    heads = range(nh)
    nt = (((1,), (1,)), ((), ()))
    kr = [pad(k_ref[0, :, h * dk:(h + 1) * dk]) for h in heads]
    qb = [pad(q_ref[0, :, h * dk:(h + 1) * dk]).astype(BF16) for h in heads]
    vb = [pad(v_ref[0, :, h * dv:(h + 1) * dv]).astype(BF16) for h in heads]
    att = [lax.dot_general(qb[h], kr[h].astype(BF16), nt, preferred_element_type=F32) for h in heads]
    o_x = [jnp.dot(qb[h], s_ref[0, h].astype(BF16), preferred_element_type=F32) for h in heads]
    o_in = [jnp.dot((att[h] * din_ref[h]).astype(BF16), vb[h], preferred_element_type=F32) for h in heads]
    upd = [jnp.dot((kr[h] * dkk_ref[h]).T.astype(BF16), vb[h], preferred_element_type=F32) for h in heads]
    for h in heads:
        s_ref[0, h] = s_ref[0, h] * dc_ref[h] + upd[h]
    for h in heads:
        o = (o_in[h] + o_x[h] * dq_ref[h])[:c]
        mu = jnp.mean(o, axis=-1, keepdims=True)
        d = o - mu
        var = jnp.mean(d * d, axis=-1, keepdims=True)
        on = d * lax.rsqrt(var + EPS) * gain_ref[:, h * dv:(h + 1) * dv]
        o_ref[0, :, h * dv:(h + 1) * dv] = (g_ref[0, :, h * dv:(h + 1) * dv] * on).astype(BF16)


def _decay_tables(nh, c, cp):
    log_g = jnp.log1p(-jnp.exp2(-5.0 - jnp.arange(nh, dtype=F32)))
    idx = jnp.arange(c, dtype=F32)
    rel = idx[:, None] - idx[None, :]
    d_in = jnp.where(rel[None] >= 0, jnp.exp(log_g[:, None, None] * jnp.maximum(rel, 0.0)[None]), 0.0)
    d_q = jnp.exp(log_g[:, None] * (idx[None, :] + 1.0))
    d_k = jnp.exp(log_g[:, None] * (c - 1.0 - idx[None, :]))
    d_c = jnp.exp(log_g * c)
    d_in = jnp.pad(d_in, ((0, 0), (0, cp - c), (0, cp - c)))
    d_q = jnp.pad(d_q, ((0, 0), (0, cp - c)))[:, :, None]
    d_k = jnp.pad(d_k, ((0, 0), (0, cp - c)))[:, :, None]
    return d_in, d_q, d_k, d_c[:, None, None]


def _rope_tables(pos, half):
    inv = ROPE_BASE ** (-jnp.arange(half, dtype=F32) / half)
    ang = pos.astype(F32)[:, None] * inv[None, :]
    return jnp.cos(ang), jnp.sin(ang)


def _retention(proj, gn_gain, s0):
    b, l, width = proj.shape
    nh = RET_HEADS
    hk = width // 6
    hv = 2 * hk
    dk, dv = hk // nh, hv // nh
    c = math.gcd(l, RET_CHUNK)
    cp = RET_CHUNK
    n = l // c
    d_in, d_q, d_k, d_c = _decay_tables(nh, c, cp)

    has_s0 = s0 is not None
    ins = [proj, proj, proj, proj, d_in, d_q, d_k, d_c, gn_gain.reshape(1, hv)]
    specs = [
        pl.BlockSpec((1, c, hk), lambda bi, i: (bi, i, 0)),
        pl.BlockSpec((1, c, hk), lambda bi, i: (bi, i, 1)),
        pl.BlockSpec((1, c, hv), lambda bi, i: (bi, i, 1)),
        pl.BlockSpec((1, c, hv), lambda bi, i: (bi, i, 2)),
        _resident(d_in.shape), _resident(d_q.shape), _resident(d_k.shape), _resident(d_c.shape),
        _resident((1, hv)),
    ]
    if has_s0:
        ins.append(s0)
        specs.append(pl.BlockSpec((1, nh, dk, dv), lambda bi, i: (bi, 0, 0, 0)))
    return pl.pallas_call(
        functools.partial(_ret_body, nh=nh, dk=dk, dv=dv, c=c, cp=cp, has_s0=has_s0),
        grid=(b, n),
        in_specs=specs,
        out_specs=[pl.BlockSpec((1, c, hv), lambda bi, i: (bi, i, 0)),
                   pl.BlockSpec((1, nh, dk, dv), lambda bi, i: (bi, 0, 0, 0))],
        out_shape=[jax.ShapeDtypeStruct((b, l, hv), BF16),
                   jax.ShapeDtypeStruct((b, nh, dk, dv), F32)],
        compiler_params=_params("parallel", "arbitrary"),
        name="retention",
    )(*ins)


def _ret_layer_body(x_ref, g_ref, win_ref, cos_ref, sin_ref, din_ref, dq_ref, dkk_ref, dc_ref, gain_ref,
                    wout_ref, o_ref, s_ref, gated_ref, *, nh, dk, dv, c):
    @pl.when(pl.program_id(1) == 0)
    def _():
        s_ref[...] = jnp.zeros(s_ref.shape, F32)

    x = x_ref[...]
    y = jnp.dot(_rms(x, g_ref[...]).astype(BF16), win_ref[...], preferred_element_type=F32)
    cos, sin = cos_ref[...], sin_ref[...]
    half, hk, hv = dk // 2, nh * dk, nh * dv

    def rope(off, scale=None):
        t1, t2 = y[:, off:off + half], y[:, off + half:off + dk]
        r1, r2 = t1 * cos - t2 * sin, t1 * sin + t2 * cos
        if scale is not None:
            r1, r2 = r1 * scale, r2 * scale
        return jnp.concatenate([r1, r2], axis=-1)

    heads = range(nh)
    nt = (((1,), (1,)), ((), ()))
    q = [rope(h * dk).astype(BF16) for h in heads]
    k = [rope(hk + h * dk, dk ** -0.5) for h in heads]
    v = [y[:, 2 * hk + h * dv:2 * hk + (h + 1) * dv].astype(BF16) for h in heads]
    for ck in range(x.shape[0] // c):
        r = slice(ck * c, (ck + 1) * c)
        att = [lax.dot_general(q[h][r], k[h][r].astype(BF16), nt, preferred_element_type=F32) for h in heads]
        o_x = [jnp.dot(q[h][r], s_ref[0, h].astype(BF16), preferred_element_type=F32) for h in heads]
        o_in = [jnp.dot((att[h] * din_ref[h]).astype(BF16), v[h][r], preferred_element_type=F32) for h in heads]
        upd = [jnp.dot((k[h][r] * dkk_ref[h]).T.astype(BF16), v[h][r], preferred_element_type=F32)
               for h in heads]
        for h in heads:
            s_ref[0, h] = s_ref[0, h] * dc_ref[h] + upd[h]
        for h in heads:
            o = o_in[h] + o_x[h] * dq_ref[h]
            mu = jnp.mean(o, axis=-1, keepdims=True)
            d = o - mu
            var = jnp.mean(d * d, axis=-1, keepdims=True)
            on = d * lax.rsqrt(var + EPS) * gain_ref[:, h * dv:(h + 1) * dv]
            gt = y[r, 2 * hk + hv + h * dv:2 * hk + hv + (h + 1) * dv]
            gated_ref[r, h * dv:(h + 1) * dv] = (gt * jax.nn.sigmoid(gt) * on).astype(BF16)
    o_ref[...] = x + jnp.dot(gated_ref[...], wout_ref[...], preferred_element_type=F32)


def _ret_layer(x, g, w_in, pos, seq, gn_gain, w_out, *, rows=256):
    n, d = x.shape
    nh = RET_HEADS
    hk = w_in.shape[1] // 6
    hv = 2 * hk
    dk, dv = hk // nh, hv // nh
    c = RET_CHUNK
    rows = min(rows, seq)
    assert seq % rows == 0 and rows % c == 0
    per_seq = seq // rows
    cos, sin = _rope_tables(pos, dk // 2)
    d_in, d_q, d_k, d_c = _decay_tables(nh, c, c)
    x_map = lambda bi, i: (bi * per_seq + i, 0)
    tab_map = lambda bi, i: (i, 0)
    return pl.pallas_call(
        functools.partial(_ret_layer_body, nh=nh, dk=dk, dv=dv, c=c),
        grid=(n // seq, per_seq),
        in_specs=[pl.BlockSpec((rows, d), x_map), _resident((1, d)), _resident(w_in.shape),
                  pl.BlockSpec((rows, dk // 2), tab_map), pl.BlockSpec((rows, dk // 2), tab_map),
                  _resident(d_in.shape), _resident(d_q.shape), _resident(d_k.shape), _resident(d_c.shape),
                  _resident((1, hv)), _resident(w_out.shape)],
        out_specs=[pl.BlockSpec((rows, d), x_map),
                   pl.BlockSpec((1, nh, dk, dv), lambda bi, i: (bi, 0, 0, 0))],
        out_shape=[jax.ShapeDtypeStruct((n, d), F32), jax.ShapeDtypeStruct((n // seq, nh, dk, dv), F32)],
        scratch_shapes=[pltpu.VMEM((rows, hv), BF16)],
        compiler_params=_params("parallel", "arbitrary"),
        name="ret_layer",
    )(x, g.reshape(1, d), w_in, cos, sin, d_in, d_q, d_k, d_c, gn_gain.reshape(1, hv), w_out)


def _qkv_body(x_ref, g_ref, w_ref, q_ref, k_ref, v_ref, kb_ref, vt_ref, km_ref, *, d, blk, scale):
    h = _rms(x_ref[...], g_ref[...]).astype(BF16)
    y = jnp.dot(h, w_ref[...], preferred_element_type=F32)
    k = y[:, d:2 * d]
    v = y[:, 2 * d:]
    q_ref[...] = (y[:, :d] * scale).astype(BF16)
    k_ref[...] = k.reshape(k_ref.shape)
    v_ref[...] = v.reshape(v_ref.shape)
    kb_ref[...] = k.astype(BF16)
    vt_ref[0] = v.T.astype(BF16)
    for r in range(k.shape[0] // blk):
        km_ref[r] = jnp.sum(k[r * blk:(r + 1) * blk], axis=0, keepdims=True) * (1.0 / blk)


def _moba_qkv(x, g, w, seq, *, tm=256):
    n, d = x.shape
    hd = d // MOBA_HEADS
    blk = MOBA_BLOCK
    tm = _row_tile(n, tm)
    assert tm % blk == 0 and seq % tm == 0
    per_seq = seq // tm
    row = lambda i: (i, 0)
    return pl.pallas_call(
        functools.partial(_qkv_body, d=d, blk=blk, scale=LOG2E * hd ** -0.5),
        grid=(n // tm,),
        in_specs=[pl.BlockSpec((tm, d), row), _resident((1, d)), _resident(w.shape)],
        out_specs=[pl.BlockSpec((tm, d), row),
                   pl.BlockSpec((tm, MOBA_HEADS, hd), lambda i: (i, 0, 0)),
                   pl.BlockSpec((tm, MOBA_HEADS, hd), lambda i: (i, 0, 0)),
                   pl.BlockSpec((tm, d), row),
                   pl.BlockSpec((1, d, tm), lambda i: (i // per_seq, 0, i % per_seq)),
                   pl.BlockSpec((tm // blk, 1, d), lambda i: (i, 0, 0))],
        out_shape=[jax.ShapeDtypeStruct((n, d), BF16), jax.ShapeDtypeStruct((n, MOBA_HEADS, hd), F32),
                   jax.ShapeDtypeStruct((n, MOBA_HEADS, hd), F32), jax.ShapeDtypeStruct((n, d), BF16),
                   jax.ShapeDtypeStruct((n // seq, d, seq), BF16),
                   jax.ShapeDtypeStruct((n // blk, 1, d), F32)],
        compiler_params=_params("parallel"),
        name="moba_qkv",
    )(x, g.reshape(1, d), w)


def _qkv_small_body(x_ref, g_ref, w_ref, q_ref, k_ref, v_ref, *, d, scale):
    h = _rms(x_ref[...], g_ref[...]).astype(BF16)
    y = jnp.dot(h, w_ref[...], preferred_element_type=F32)
    q_ref[...] = y[:, :d] * scale
    k_ref[...] = y[:, d:2 * d]
    v_ref[...] = y[:, 2 * d:]


def _moba_qkv_small(x, g, w):
    n, d = x.shape
    hd = d // MOBA_HEADS
    full = pl.BlockSpec((n, d), lambda i: (0, 0))
    return pl.pallas_call(
        functools.partial(_qkv_small_body, d=d, scale=hd ** -0.5),
        grid=(1,),
        in_specs=[full, _resident((1, d)), _resident(w.shape)],
        out_specs=[full, full, full],
        out_shape=[jax.ShapeDtypeStruct((n, d), F32)] * 3,
        compiler_params=_params("arbitrary"),
        name="moba_qkv_small",
    )(x, g.reshape(1, d), w)


def _t5_bucket_np(dist):
    n = np.maximum(dist, 0)
    max_exact = REL_BUCKETS // 2
    nf = np.maximum(n, 1).astype(np.float64)
    large = max_exact + (np.log(nf / max_exact) / math.log(REL_MAX_DIST / max_exact)
                         * (REL_BUCKETS - max_exact)).astype(np.int64)
    large = np.minimum(large, REL_BUCKETS - 1)
    return np.where(n < max_exact, n, large).astype(np.int32)


def _bias_body(bk_ref, tab_ref, o_ref, *, scale):
    bk = bk_ref[0]
    acc = jnp.where(bk < 0, NEG_INF, 0.0).astype(F32)
    for b in range(REL_BUCKETS):
        acc = jnp.where(bk == b, tab_ref[0, b] * scale, acc)
    o_ref[0, 0] = acc


def _bias_tiles(buckets, table, scale=1.0):
    t, r, w = buckets.shape
    g, nbuck, tr, tw = table.shape
    assert nbuck == REL_BUCKETS and tr in (1, r) and tw in (1, w)
    return pl.pallas_call(
        functools.partial(_bias_body, scale=scale),
        grid=(g, t),
        in_specs=[pl.BlockSpec((1, r, w), lambda gi, ti: (ti, 0, 0)),
                  pl.BlockSpec((1, REL_BUCKETS, tr, tw), lambda gi, ti: (gi, 0, 0, 0))],
        out_specs=pl.BlockSpec((1, 1, r, w), lambda gi, ti: (gi, ti, 0, 0)),
        out_shape=jax.ShapeDtypeStruct((g, t, r, w), F32),
        compiler_params=_params("parallel", "parallel"),
        name="t5_bias_tiles",
    )(jnp.asarray(buckets), table)


FLASH_HEADS = 8
DEN_ROWS = 16


def _flash_body(q_ref, k_ref, vt_ref, km_ref, bias_ref, o_ref, *, nb, blk, hd, topk, nch):
    pair = 2 * hd
    lane = lax.broadcasted_iota(jnp.int32, (1, pair), 1)
    row = lax.broadcasted_iota(jnp.int32, (pair, 1), 0)
    bidx = lax.broadcasted_iota(jnp.int32, (nb, 1), 0)
    nt = (((1,), (1,)), ((), ()))

    wq = 2 * blk

    def lanes(pc):
        return slice(pc * pair, (pc + 1) * pair)

    def both(pc, idx):
        return jnp.concatenate([bias_ref[(2 * pc,) + idx], bias_ref[(2 * pc + 1,) + idx]], axis=1)

    def masked_q(pc):
        qp = q_ref[0, :, lanes(pc)]
        zero = jnp.zeros((blk, pair), BF16)
        return jnp.concatenate([jnp.where(lane < hd, qp, zero), jnp.where(lane >= hd, qp, zero)], axis=0)

    def q_block(t):
        npair = nch // 2
        qs = [masked_q(pc) for pc in range(npair)]
        adds = []
        for pc in range(npair):
            add = [None] * (t + 1)
            add[t] = both(pc, (0,))
            if t >= 1:
                gate = lax.dot_general(km_ref[0, :, lanes(pc)].astype(BF16), qs[pc], nt,
                                       preferred_element_type=F32)
                valid = bidx < t
                gate = jnp.where(valid, gate, NEG_INF)
                rank = jnp.zeros((nb, wq), jnp.int32)
                for j2 in range(t):
                    r2 = gate[j2:j2 + 1, :]
                    ahead = (r2 > gate) | ((r2 == gate) & (j2 < bidx))
                    rank = rank + ahead.astype(jnp.int32)
                sel = valid & (rank < topk)
                add[t - 1] = both(pc, (1,)) + jnp.where(sel[t - 1:t], 0.0, NEG_INF)
                far = jnp.where(sel, both(pc, (2, slice(0, 1))), NEG_INF)
                for j in range(t - 1):
                    add[j] = far[j:j + 1]
            adds.append(add)

        order = [t] + list(range(t))

        def qk(pc, j):
            return lax.dot_general(k_ref[0, j * blk:(j + 1) * blk, lanes(pc)], qs[pc], nt,
                                   preferred_element_type=F32)

        ones = jnp.ones((DEN_ROWS, blk), BF16)

        def pv(pc, j, p):
            vt1 = jnp.concatenate([vt_ref[0, lanes(pc), j * blk:(j + 1) * blk], ones], axis=0)
            return jnp.dot(vt1, p, preferred_element_type=F32)

        state = [dict(x={}, m=None, acc=None) for _ in range(npair)]

        def pass1(pc):
            st = state[pc]
            pend = None
            for j in order:
                s = qk(pc, j)
                yield
                if pend is not None:
                    finish(st, pc, *pend)
                pend = (j, s)
            finish(st, pc, *pend)
            yield

        def finish(st, pc, j, s):
            add = adds[pc][j]
            if add.shape[0] == 1:
                st["x"][j] = s
                mx = s.max(axis=0, keepdims=True) + add
            else:
                st["x"][j] = s + add
                mx = st["x"][j].max(axis=0, keepdims=True)
            st["m"] = mx if st["m"] is None else jnp.maximum(st["m"], mx)

        def pass2(pc):
            st = state[pc]
            pend = None
            for j in order:
                add = adds[pc][j]
                shift = st["m"] - add if add.shape[0] == 1 else st["m"]
                p = jnp.exp2(st["x"][j] - shift).astype(BF16)
                yield
                if pend is not None:
                    a = pv(pc, *pend)
                    st["acc"] = a if st["acc"] is None else st["acc"] + a
                pend = (j, p)
            a = pv(pc, *pend)
            st["acc"] = a if st["acc"] is None else st["acc"] + a
            yield

        def run_together(*gens):
            gens = list(gens)
            while gens:
                for g in list(gens):
                    try:
                        next(g)
                    except StopIteration:
                        gens.remove(g)

        run_together(pass1(0))
        for pc in range(npair):
            if pc + 1 < npair:
                run_together(pass1(pc + 1), pass2(pc))
            else:
                run_together(pass2(pc))
        for pc in range(npair):
            st = state[pc]
            acc = st["acc"][:pair] / st["acc"][pair:pair + 1]
            out_t = jnp.where(row >= hd, acc[:, blk:], acc[:, :blk])
            o_ref[0, :, lanes(pc)] = out_t.T.astype(BF16)

    for t in range(nb):
        pl.when(pl.program_id(2) == t)(functools.partial(q_block, t))


def _moba_flash(q, kb, vt, kmean, bias):
    b, l, d = q.shape
    hd = d // MOBA_HEADS
    pair = 2 * hd
    assert pair == LANES
    blk = MOBA_BLOCK
    nb = l // blk
    nch = FLASH_HEADS
    w = nch * hd
    assert nch % 2 == 0 and d % w == 0
    return pl.pallas_call(
        functools.partial(_flash_body, nb=nb, blk=blk, hd=hd, topk=MOBA_TOPK, nch=nch),
        grid=(b, d // w, nb),
        in_specs=[pl.BlockSpec((1, blk, w), lambda bi, p, qi: (bi, qi, p)),
                  pl.BlockSpec((1, l, w), lambda bi, p, qi: (bi, 0, p)),
                  pl.BlockSpec((1, w, l), lambda bi, p, qi: (bi, p, 0)),
                  pl.BlockSpec((1, nb, w), lambda bi, p, qi: (bi, 0, p)),
                  _resident(bias.shape) if d == w else
                  pl.BlockSpec((nch, 3, blk, blk), lambda bi, p, qi: (p, 0, 0, 0))],
        out_specs=pl.BlockSpec((1, blk, w), lambda bi, p, qi: (bi, qi, p)),
        out_shape=jax.ShapeDtypeStruct((b, l, d), BF16),
        compiler_params=_params("parallel", "parallel", "arbitrary"),
        name="moba_flash",
    )(q, kb, vt, kmean, bias)


def _prompt_bias_buckets(blk):
    assert REL_MAX_DIST <= blk + 1
    key = np.arange(blk)[:, None]
    qry = np.arange(blk)[None, :]
    tiles = []
    for dblk in range(3):
        dist = dblk * blk + qry - key
        tiles.append(np.where(dist >= 0, _t5_bucket_np(dist), -1))
    return np.stack(tiles).astype(np.int32)


SAMPLE_BLOCKS_PER_STEP = 8


def _sample_body(pt_ref, bsel_ref, qbdt_ref, *refs, nbp, bps, heads, lq, topk):
    npg = 2 * bps
    k_refs, v_refs = refs[:npg], refs[npg:2 * npg]
    kn_ref, vn_ref, bias_ref, bias_own_ref, o_ref, m_ref, l_ref, g_ref, acc_ref = refs[2 * npg:]
    step = pl.program_id(1)
    hq = heads * lq
    qbdt = qbdt_ref[0]
    d = qbdt.shape[1]
    hd = d // heads
    lane = lax.broadcasted_iota(jnp.int32, (1, LANES), 1)
    nt = (((1,), (1,)), ((), ()))

    @pl.when(step == 0)
    def _():
        m_ref[...] = jnp.zeros(m_ref.shape, F32)
        l_ref[...] = jnp.zeros(l_ref.shape, F32)
        g_ref[...] = jnp.zeros(g_ref.shape, F32)

    def block(k, v, bias, token_major=False):
        kb, vb = k.astype(BF16), v.astype(BF16)
        if token_major:
            sr = lax.dot_general(qbdt, kb, nt, preferred_element_type=F32)
        else:
            sr = jnp.dot(qbdt, kb, preferred_element_type=F32)
        g = jnp.sum(sr, axis=1, keepdims=True) * (1.0 / sr.shape[1])
        sc = sr + bias
        m = jnp.max(sc, axis=1, keepdims=True)
        p = jnp.exp(sc - m)
        l = jnp.sum(p, axis=1, keepdims=True)
        if token_major:
            acc = jnp.dot(p.astype(BF16), vb, preferred_element_type=F32)
        else:
            acc = lax.dot_general(p.astype(BF16), vb, nt, preferred_element_type=F32)
        slab = jnp.concatenate(
            [acc[h * lq:(h + 1) * lq, (h * hd // LANES) * LANES:(h * hd // LANES + 1) * LANES]
             for h in range(heads)], axis=0)
        return m, l, g, slab

    stats = []
    for jj in range(bps):
        j = step * bps + jj
        kt = jnp.concatenate([k_refs[2 * jj][0], k_refs[2 * jj + 1][0]], axis=1)
        vt = jnp.concatenate([v_refs[2 * jj][0], v_refs[2 * jj + 1][0]], axis=1)
        m, l, g, slab = block(kt, vt, bias_ref[bsel_ref[j]])
        acc_ref[j] = slab
        stats.append((j, m, l, g))
    m_all, l_all, g_all = m_ref[...], l_ref[...], g_ref[...]
    for j, m, l, g in stats:
        hit = lane == j
        m_all = jnp.where(hit, m, m_all)
        l_all = jnp.where(hit, l, l_all)
        g_all = jnp.where(hit, g, g_all)
    m_ref[...] = m_all
    l_ref[...] = l_all
    g_ref[...] = g_all

    @pl.when(step == nbp // bps - 1)
    def _():
        zpad = jnp.zeros((PAGE_SIZE - lq, d), F32)
        m, l, _, slab = block(jnp.concatenate([kn_ref[0], zpad], axis=0),
                              jnp.concatenate([vn_ref[0], zpad], axis=0), bias_own_ref[...], token_major=True)
        acc_ref[nbp] = slab
        own = lane == nbp
        m_fin = jnp.where(own, m, m_all)
        l_fin = jnp.where(own, l, l_all)
        lane_f = lane.astype(F32)
        gate = jnp.where(lane < nbp, g_all, NEG_INF)
        sel = jnp.broadcast_to(own, (hq, LANES))
        for _ in range(min(topk, nbp)):
            best = jnp.max(gate, axis=1, keepdims=True)
            first = jnp.min(jnp.where(gate == best, lane_f, float(LANES)), axis=1, keepdims=True)
            pick = lane_f == first
            sel = sel | pick
            gate = jnp.where(pick, NEG_INF, gate)
        top = jnp.max(jnp.where(sel, m_fin, NEG_INF), axis=1, keepdims=True)
        w = jnp.where(sel, jnp.exp(m_fin - top), 0.0)
        wn = w / jnp.sum(w * l_fin, axis=1, keepdims=True)
        out = jnp.zeros((hq, LANES), F32)
        for jj in range(nbp + 1):
            out = out + wn[:, jj:jj + 1] * acc_ref[jj]
        rrow = lax.broadcasted_iota(jnp.int32, (hq, 1), 0)
        keep = (lane // hd) == ((rrow // lq) % (LANES // hd))
        o_ref[0] = jnp.where(keep, out, 0.0)


def _moba_sample(q, k_new, v_new, cache_kt, cache_vt, page_base, page_table, rel_bias):
    b, lq, d = q.shape
    heads = MOBA_HEADS
    hd = d // heads
    hq = heads * lq
    blk = MOBA_BLOCK
    n_pages = page_table.shape[1]
    past = n_pages * PAGE_SIZE
    assert hq == LANES and blk == 2 * PAGE_SIZE and past % blk == 0 and lq <= PAGE_SIZE
    nbp = past // blk
    bps = math.gcd(nbp, SAMPLE_BLOCKS_PER_STEP)
    assert nbp + 1 <= LANES

    same_head = (np.arange(hq)[:, None] // lq) == (np.arange(d)[None, :] // hd)
    qbdt = jnp.where(jnp.asarray(same_head), jnp.tile(q, (1, heads, 1)), 0.0).astype(BF16)

    qry = np.tile(np.arange(lq), heads)[:, None]
    key = np.arange(blk)[None, :]
    tiles, bsel = [], []
    for jb in range(nbp):
        dist = past + qry - (jb * blk + key)
        bk = np.where(dist >= 0, _t5_bucket_np(dist), -1).astype(np.int32)
        for t, existing in enumerate(tiles):
            if np.array_equal(existing, bk):
                bsel.append(t)
                break
        else:
            tiles.append(bk)
            bsel.append(len(tiles) - 1)
    table = jnp.repeat(rel_bias, lq, axis=1)[None, :, :, None]
    bias = _bias_tiles(np.stack(tiles), table)[0]
    dist_own = qry - np.arange(PAGE_SIZE)[None, :]
    own_bk = np.where(dist_own >= 0, _t5_bucket_np(dist_own), -1).astype(np.int32)
    bias_own = _bias_tiles(own_bk[None], table)[0, 0]
    nt = bias.shape[0]

    npg = 2 * bps

    def page(t):
        return lambda bi, s, pt, bs: (page_base + pt[bi * n_pages + s * npg + t], 0, 0)

    per_b = lambda bi, s, pt, bs: (bi, 0, 0)
    page_specs = [pl.BlockSpec((1, d, PAGE_SIZE), page(t)) for t in range(npg)]
    grid_spec = pltpu.PrefetchScalarGridSpec(
        num_scalar_prefetch=2,
        grid=(b, nbp // bps),
        in_specs=[pl.BlockSpec((1, hq, d), per_b)] + page_specs + page_specs + [
            pl.BlockSpec((1, lq, d), per_b), pl.BlockSpec((1, lq, d), per_b),
            pl.BlockSpec((nt, hq, blk), lambda bi, s, pt, bs: (0, 0, 0)),
            pl.BlockSpec((hq, PAGE_SIZE), lambda bi, s, pt, bs: (0, 0))],
        out_specs=pl.BlockSpec((1, hq, LANES), per_b),
        scratch_shapes=[pltpu.VMEM((hq, LANES), F32), pltpu.VMEM((hq, LANES), F32),
                        pltpu.VMEM((hq, LANES), F32), pltpu.VMEM((nbp + 1, hq, LANES), F32)],
    )
    o2 = pl.pallas_call(
        functools.partial(_sample_body, nbp=nbp, bps=bps, heads=heads, lq=lq, topk=MOBA_TOPK),
        grid_spec=grid_spec,
        out_shape=jax.ShapeDtypeStruct((b, hq, LANES), F32),
        compiler_params=_params("parallel", "arbitrary"),
        name="moba_sample",
    )(page_table.reshape(-1), jnp.asarray(np.array(bsel, np.int32)), qbdt,
      *([cache_kt] * npg), *([cache_vt] * npg), k_new, v_new, bias, bias_own)
    o = o2.reshape(b, heads, lq, LANES // hd, hd).sum(axis=3)
    return o.transpose(0, 2, 1, 3).reshape(b, lq, d).astype(BF16)


def kernel(x_prompt, x_sample, state_ret, cache_k, cache_v, page_table, ffn1_norm, ffn1_w_in, ffn1_w_out,
           mix_norm, ffn2_norm, ffn2_w_in, ffn2_w_out, ret_w_in, ret_gn_gain, ret_w_out, moba_w_qkv,
           moba_w_out, rel_bias, final_norm):
    bp, lp, d = x_prompt.shape
    bs, ls, _ = x_sample.shape
    depth = ffn1_norm.shape[0]
    past = page_table.shape[1] * PAGE_SIZE
    pos_p = jnp.arange(lp, dtype=jnp.int32)
    pos_s = past + jnp.arange(ls, dtype=jnp.int32)
    hp = x_prompt.reshape(bp * lp, d)
    hs = x_sample.reshape(bs * ls, d)
    cast = lambda w: w.astype(BF16)
    w1i, w1o, w2i, w2o = cast(ffn1_w_in), cast(ffn1_w_out), cast(ffn2_w_in), cast(ffn2_w_out)
    ret_p, ret_s, kp_l, vp_l, ks_l, vs_l = [], [], [], [], [], []
    for i in range(depth):
        hp = _ffn(hp, ffn1_norm[i], w1i, w1o, i)
        hs = _ffn(hs, ffn1_norm[i], w1i, w1o, i)
        if i % 2 == 0:
            r = i // 2
            w_in, w_out = cast(ret_w_in[r]), cast(ret_w_out[r])
            if lp % RET_CHUNK == 0:
                hp, sp = _ret_layer(hp, mix_norm[i], w_in, pos_p, lp, ret_gn_gain[r], w_out)
            else:
                proj_p = _ret_proj(hp, mix_norm[i], w_in, pos_p, lp).reshape(bp, lp, -1)
                gp, sp = _retention(proj_p, ret_gn_gain[r], None)
                hp = _proj_res(gp.reshape(bp * lp, -1), w_out, hp)
            proj_s = _ret_proj(hs, mix_norm[i], w_in, pos_s, ls).reshape(bs, ls, -1)
            gs, ss = _retention(proj_s, ret_gn_gain[r], state_ret[r])
            hs = _proj_res(gs.reshape(bs * ls, -1), w_out, hs)
            ret_p.append(sp)
            ret_s.append(ss)
        else:
            m = i // 2
            w_qkv, w_out = cast(moba_w_qkv[m]), cast(moba_w_out[m])
            hd = d // MOBA_HEADS
            q, k, v, kb, vt, km = _moba_qkv(hp, mix_norm[i], w_qkv, lp)
            bias = _bias_tiles(_prompt_bias_buckets(MOBA_BLOCK), rel_bias.T[:, :, None, None], LOG2E)
            op = _moba_flash(q.reshape(bp, lp, d), kb.reshape(bp, lp, d), vt,
                             km.reshape(bp, lp // MOBA_BLOCK, d), bias)
            hp = _proj_res(op.reshape(bp * lp, d), w_out, hp)
            kp_l.append(k.reshape(bp, lp, MOBA_HEADS, hd))
            vp_l.append(v.reshape(bp, lp, MOBA_HEADS, hd))

            qs, ksn, vsn = _moba_qkv_small(hs, mix_norm[i], w_qkv)
            n_pool = cache_k.shape[1]
            pages_t = lambda cache: cache.transpose(0, 1, 3, 4, 2).reshape(-1, d, PAGE_SIZE)
            os_ = _moba_sample(qs.reshape(bs, ls, d), ksn.reshape(bs, ls, d), vsn.reshape(bs, ls, d),
                               pages_t(cache_k), pages_t(cache_v), m * n_pool, page_table, rel_bias)
            hs = _proj_res(os_.reshape(bs * ls, d), w_out, hs)
            ks_l.append(ksn.reshape(bs, ls, MOBA_HEADS, hd))
            vs_l.append(vsn.reshape(bs, ls, MOBA_HEADS, hd))
        last = i == depth - 1
        hp = _ffn(hp, ffn2_norm[i], w2i, w2o, i, final_norm if last else None)
        hs = _ffn(hs, ffn2_norm[i], w2i, w2o, i, final_norm if last else None)
    return (hp.reshape(bp, lp, d), hs.reshape(bs, ls, d), jnp.stack(ret_p), jnp.stack(ret_s),
            jnp.stack(kp_l), jnp.stack(vp_l), jnp.stack(ks_l), jnp.stack(vs_l))
```

```python
import functools
import math

import jax
import jax.numpy as jnp
import numpy as np
from jax import lax
from jax.experimental import pallas as pl
from jax.experimental.pallas import tpu as pltpu

F32 = jnp.float32
BF16 = jnp.bfloat16
NEG_INF = float("-inf")
LOG2E = math.log2(math.e)

LANES = 128
SUBLANES = 8
VMEM_LIMIT_BYTES = 56 * 1024 * 1024

EPS = 1e-6
ROPE_BASE = 10000.0
RET_HEADS = 4
RET_CHUNK = 128
MOBA_HEADS = 16
MOBA_BLOCK = 256
MOBA_TOPK = 3
PAGE_SIZE = 128
REL_BUCKETS = 32
REL_MAX_DIST = 128


def _params(*semantics):
    return pltpu.CompilerParams(dimension_semantics=semantics, vmem_limit_bytes=VMEM_LIMIT_BYTES)


def _resident(shape):
    zeros = (0,) * len(shape)
    return pl.BlockSpec(shape, lambda *_: zeros, pipeline_mode=pl.Buffered(1))


def _resident_layer(stacked, layer):
    idx = (layer,) + (0,) * (stacked.ndim - 1)
    return pl.BlockSpec((None,) + stacked.shape[1:], lambda *_: idx, pipeline_mode=pl.Buffered(1))


def _row_tile(n, want):
    t = min(n, want)
    assert n % t == 0, (n, t)
    return t


def _rms(x, g):
    y = x * lax.rsqrt(jnp.mean(x * x, axis=-1, keepdims=True) + EPS)
    return y * g


def _ffn_body(*refs, d_ff, tf, final):
    if final:
        x_ref, g_ref, win_ref, wout_ref, fg_ref, o_ref = refs
    else:
        x_ref, g_ref, win_ref, wout_ref, o_ref = refs
    x = x_ref[...]
    h = _rms(x, g_ref[...]).astype(BF16)
    y = None
    for c in range(d_ff // tf):
        a = jnp.dot(h, win_ref[:, c * tf:(c + 1) * tf], preferred_element_type=F32)
        b = jnp.dot(h, win_ref[:, d_ff + c * tf:d_ff + (c + 1) * tf], preferred_element_type=F32)
        gated = (a * jax.nn.sigmoid(a) * b).astype(BF16)
        part = jnp.dot(gated, wout_ref[c * tf:(c + 1) * tf, :], preferred_element_type=F32)
        y = part if y is None else y + part
    out = x + 0.5 * y
    if final:
        out = _rms(out, fg_ref[...])
    o_ref[...] = out


MXU_TILE = 256


def _ffn(x, g, w_in, w_out, layer, final_g=None, *, tm=512):
    n, d = x.shape
    d_ff = w_out.shape[1]
    tm = _row_tile(n, tm)
    tf = d_ff if d_ff % MXU_TILE == 0 else LANES
    assert d_ff % tf == 0
    final = final_g is not None
    ins = [x, g.reshape(1, d), w_in, w_out]
    specs = [pl.BlockSpec((tm, d), lambda i: (i, 0)), _resident((1, d)),
             _resident_layer(w_in, layer), _resident_layer(w_out, layer)]
    if final:
        ins.append(final_g.reshape(1, d))
        specs.append(_resident((1, d)))
    return pl.pallas_call(
        functools.partial(_ffn_body, d_ff=d_ff, tf=tf, final=final),
        grid=(n // tm,),
        in_specs=specs,
        out_specs=pl.BlockSpec((tm, d), lambda i: (i, 0)),
        out_shape=jax.ShapeDtypeStruct((n, d), F32),
        compiler_params=_params("parallel"),
        name="ffn",
    )(*ins)


def _ret_proj_body(x_ref, g_ref, w_ref, cos_ref, sin_ref, o_ref, *, nh, dk, dv):
    h = _rms(x_ref[...], g_ref[...]).astype(BF16)
    y = jnp.dot(h, w_ref[...], preferred_element_type=F32)
    cos, sin = cos_ref[...], sin_ref[...]
    half, hk, hv = dk // 2, nh * dk, nh * dv
    for part in range(2):
        for hh in range(nh):
            off = part * hk + hh * dk
            t1, t2 = y[:, off:off + half], y[:, off + half:off + dk]
            r1, r2 = t1 * cos - t2 * sin, t1 * sin + t2 * cos
            if part == 1:
                r1, r2 = r1 * (dk ** -0.5), r2 * (dk ** -0.5)
            o_ref[:, off:off + half] = r1
            o_ref[:, off + half:off + dk] = r2
    o_ref[:, 2 * hk:2 * hk + hv] = y[:, 2 * hk:2 * hk + hv]
    gt = y[:, 2 * hk + hv:]
    o_ref[:, 2 * hk + hv:] = gt * jax.nn.sigmoid(gt)


def _ret_proj(x, g, w, pos, seq, *, tm=256):
    n, d = x.shape
    nout = w.shape[1]
    nh = RET_HEADS
    dk = nout // 6 // nh
    dv = 2 * dk
    half = dk // 2
    tm = _row_tile(n, tm)
    cos, sin = _rope_tables(pos, half)
    if seq >= tm:
        assert seq % tm == 0
        per_seq = seq // tm
        table_map = lambda i: (i % per_seq, 0)
    else:
        assert tm % seq == 0
        cos, sin = jnp.tile(cos, (tm // seq, 1)), jnp.tile(sin, (tm // seq, 1))
        table_map = lambda i: (0, 0)
    return pl.pallas_call(
        functools.partial(_ret_proj_body, nh=nh, dk=dk, dv=dv),
        grid=(n // tm,),
        in_specs=[pl.BlockSpec((tm, d), lambda i: (i, 0)), _resident((1, d)), _resident(w.shape),
                  pl.BlockSpec((tm, half), table_map), pl.BlockSpec((tm, half), table_map)],
        out_specs=pl.BlockSpec((tm, nout), lambda i: (i, 0)),
        out_shape=jax.ShapeDtypeStruct((n, nout), F32),
        compiler_params=_params("parallel"),
        name="ret_proj",
    )(x, g.reshape(1, d), w, cos, sin)


def _proj_res_body(x_ref, w_ref, r_ref, o_ref):
    o_ref[...] = r_ref[...] + jnp.dot(x_ref[...], w_ref[...], preferred_element_type=F32)


def _proj_res(x, w, res, *, tm=512):
    n, k = x.shape
    d = w.shape[1]
    tm = _row_tile(n, tm)
    return pl.pallas_call(
        _proj_res_body,
        grid=(n // tm,),
        in_specs=[pl.BlockSpec((tm, k), lambda i: (i, 0)), _resident(w.shape),
                  pl.BlockSpec((tm, d), lambda i: (i, 0))],
        out_specs=pl.BlockSpec((tm, d), lambda i: (i, 0)),
        out_shape=jax.ShapeDtypeStruct((n, d), F32),
        compiler_params=_params("parallel"),
        name="proj_res",
    )(x, w, res)


def _ret_body(*refs, nh, dk, dv, c, cp, has_s0):
    if has_s0:
        (q_ref, k_ref, v_ref, g_ref, din_ref, dq_ref, dkk_ref, dc_ref, gain_ref, s0_ref, o_ref, s_ref) = refs
    else:
        (q_ref, k_ref, v_ref, g_ref, din_ref, dq_ref, dkk_ref, dc_ref, gain_ref, o_ref, s_ref) = refs
    i = pl.program_id(1)

    @pl.when(i == 0)
    def _():
        if has_s0:
            s_ref[...] = s0_ref[...]
        else:
            s_ref[...] = jnp.zeros(s_ref.shape, F32)

    def pad(t):
        if cp == c:
            return t
        return jnp.concatenate([t, jnp.zeros((cp - c, t.shape[1]), t.dtype)], axis=0)

    heads = range(nh)
    nt = (((1,), (1,)), ((), ()))
    kr = [pad(k_ref[0, :, h * dk:(h + 1) * dk]) for h in heads]
    qb = [pad(q_ref[0, :, h * dk:(h + 1) * dk]).astype(BF16) for h in heads]
    vb = [pad(v_ref[0, :, h * dv:(h + 1) * dv]).astype(BF16) for h in heads]
    att = [lax.dot_general(qb[h], kr[h].astype(BF16), nt, preferred_element_type=F32) for h in heads]
    o_x = [jnp.dot(qb[h], s_ref[0, h].astype(BF16), preferred_element_type=F32) for h in heads]
    o_in = [jnp.dot((att[h] * din_ref[h]).astype(BF16), vb[h], preferred_element_type=F32) for h in heads]
    upd = [jnp.dot((kr[h] * dkk_ref[h]).T.astype(BF16), vb[h], preferred_element_type=F32) for h in heads]
    for h in heads:
        s_ref[0, h] = s_ref[0, h] * dc_ref[h] + upd[h]
    for h in heads:
        o = (o_in[h] + o_x[h] * dq_ref[h])[:c]
        mu = jnp.mean(o, axis=-1, keepdims=True)
        d = o - mu
        var = jnp.mean(d * d, axis=-1, keepdims=True)
        on = d * lax.rsqrt(var + EPS) * gain_ref[:, h * dv:(h + 1) * dv]
        o_ref[0, :, h * dv:(h + 1) * dv] = (g_ref[0, :, h * dv:(h + 1) * dv] * on).astype(BF16)


def _decay_tables(nh, c, cp):
    log_g = jnp.log1p(-jnp.exp2(-5.0 - jnp.arange(nh, dtype=F32)))
    idx = jnp.arange(c, dtype=F32)
    rel = idx[:, None] - idx[None, :]
    d_in = jnp.where(rel[None] >= 0, jnp.exp(log_g[:, None, None] * jnp.maximum(rel, 0.0)[None]), 0.0)
    d_q = jnp.exp(log_g[:, None] * (idx[None, :] + 1.0))
    d_k = jnp.exp(log_g[:, None] * (c - 1.0 - idx[None, :]))
    d_c = jnp.exp(log_g * c)
    d_in = jnp.pad(d_in, ((0, 0), (0, cp - c), (0, cp - c)))
    d_q = jnp.pad(d_q, ((0, 0), (0, cp - c)))[:, :, None]
    d_k = jnp.pad(d_k, ((0, 0), (0, cp - c)))[:, :, None]
    return d_in, d_q, d_k, d_c[:, None, None]


def _rope_tables(pos, half):
    inv = ROPE_BASE ** (-jnp.arange(half, dtype=F32) / half)
    ang = pos.astype(F32)[:, None] * inv[None, :]
    return jnp.cos(ang), jnp.sin(ang)


def _retention(proj, gn_gain, s0):
    b, l, width = proj.shape
    nh = RET_HEADS
    hk = width // 6
    hv = 2 * hk
    dk, dv = hk // nh, hv // nh
    c = math.gcd(l, RET_CHUNK)
    cp = RET_CHUNK
    n = l // c
    d_in, d_q, d_k, d_c = _decay_tables(nh, c, cp)

    has_s0 = s0 is not None
    ins = [proj, proj, proj, proj, d_in, d_q, d_k, d_c, gn_gain.reshape(1, hv)]
    specs = [
        pl.BlockSpec((1, c, hk), lambda bi, i: (bi, i, 0)),
        pl.BlockSpec((1, c, hk), lambda bi, i: (bi, i, 1)),
        pl.BlockSpec((1, c, hv), lambda bi, i: (bi, i, 1)),
        pl.BlockSpec((1, c, hv), lambda bi, i: (bi, i, 2)),
        _resident(d_in.shape), _resident(d_q.shape), _resident(d_k.shape), _resident(d_c.shape),
        _resident((1, hv)),
    ]
    if has_s0:
        ins.append(s0)
        specs.append(pl.BlockSpec((1, nh, dk, dv), lambda bi, i: (bi, 0, 0, 0)))
    return pl.pallas_call(
        functools.partial(_ret_body, nh=nh, dk=dk, dv=dv, c=c, cp=cp, has_s0=has_s0),
        grid=(b, n),
        in_specs=specs,
        out_specs=[pl.BlockSpec((1, c, hv), lambda bi, i: (bi, i, 0)),
                   pl.BlockSpec((1, nh, dk, dv), lambda bi, i: (bi, 0, 0, 0))],
        out_shape=[jax.ShapeDtypeStruct((b, l, hv), BF16),
                   jax.ShapeDtypeStruct((b, nh, dk, dv), F32)],
        compiler_params=_params("parallel", "arbitrary"),
        name="retention",
    )(*ins)


def _ret_layer_body(x_ref, g_ref, win_ref, cos_ref, sin_ref, din_ref, dq_ref, dkk_ref, dc_ref, gain_ref,
                    wout_ref, o_ref, s_ref, gated_ref, *, nh, dk, dv, c):
    @pl.when(pl.program_id(1) == 0)
    def _():
        s_ref[...] = jnp.zeros(s_ref.shape, F32)

    x = x_ref[...]
    y = jnp.dot(_rms(x, g_ref[...]).astype(BF16), win_ref[...], preferred_element_type=F32)
    cos, sin = cos_ref[...], sin_ref[...]
    half, hk, hv = dk // 2, nh * dk, nh * dv

    def rope(off, scale=None):
        t1, t2 = y[:, off:off + half], y[:, off + half:off + dk]
        r1, r2 = t1 * cos - t2 * sin, t1 * sin + t2 * cos
        if scale is not None:
            r1, r2 = r1 * scale, r2 * scale
        return jnp.concatenate([r1, r2], axis=-1)

    heads = range(nh)
    nt = (((1,), (1,)), ((), ()))
    q = [rope(h * dk).astype(BF16) for h in heads]
    k = [rope(hk + h * dk, dk ** -0.5) for h in heads]
    v = [y[:, 2 * hk + h * dv:2 * hk + (h + 1) * dv].astype(BF16) for h in heads]
    for ck in range(x.shape[0] // c):
        r = slice(ck * c, (ck + 1) * c)
        att = [lax.dot_general(q[h][r], k[h][r].astype(BF16), nt, preferred_element_type=F32) for h in heads]
        o_x = [jnp.dot(q[h][r], s_ref[0, h].astype(BF16), preferred_element_type=F32) for h in heads]
        o_in = [jnp.dot((att[h] * din_ref[h]).astype(BF16), v[h][r], preferred_element_type=F32) for h in heads]
        upd = [jnp.dot((k[h][r] * dkk_ref[h]).T.astype(BF16), v[h][r], preferred_element_type=F32)
               for h in heads]
        for h in heads:
            s_ref[0, h] = s_ref[0, h] * dc_ref[h] + upd[h]
        for h in heads:
            o = o_in[h] + o_x[h] * dq_ref[h]
            mu = jnp.mean(o, axis=-1, keepdims=True)
            d = o - mu
            var = jnp.mean(d * d, axis=-1, keepdims=True)
            on = d * lax.rsqrt(var + EPS) * gain_ref[:, h * dv:(h + 1) * dv]
            gt = y[r, 2 * hk + hv + h * dv:2 * hk + hv + (h + 1) * dv]
            gated_ref[r, h * dv:(h + 1) * dv] = (gt * jax.nn.sigmoid(gt) * on).astype(BF16)
    o_ref[...] = x + jnp.dot(gated_ref[...], wout_ref[...], preferred_element_type=F32)


def _ret_layer(x, g, w_in, pos, seq, gn_gain, w_out, *, rows=256):
    n, d = x.shape
    nh = RET_HEADS
    hk = w_in.shape[1] // 6
    hv = 2 * hk
    dk, dv = hk // nh, hv // nh
    c = RET_CHUNK
    rows = min(rows, seq)
    assert seq % rows == 0 and rows % c == 0
    per_seq = seq // rows
    cos, sin = _rope_tables(pos, dk // 2)
    d_in, d_q, d_k, d_c = _decay_tables(nh, c, c)
    x_map = lambda bi, i: (bi * per_seq + i, 0)
    tab_map = lambda bi, i: (i, 0)
    return pl.pallas_call(
        functools.partial(_ret_layer_body, nh=nh, dk=dk, dv=dv, c=c),
        grid=(n // seq, per_seq),
        in_specs=[pl.BlockSpec((rows, d), x_map), _resident((1, d)), _resident(w_in.shape),
                  pl.BlockSpec((rows, dk // 2), tab_map), pl.BlockSpec((rows, dk // 2), tab_map),
                  _resident(d_in.shape), _resident(d_q.shape), _resident(d_k.shape), _resident(d_c.shape),
                  _resident((1, hv)), _resident(w_out.shape)],
        out_specs=[pl.BlockSpec((rows, d), x_map),
                   pl.BlockSpec((1, nh, dk, dv), lambda bi, i: (bi, 0, 0, 0))],
        out_shape=[jax.ShapeDtypeStruct((n, d), F32), jax.ShapeDtypeStruct((n // seq, nh, dk, dv), F32)],
        scratch_shapes=[pltpu.VMEM((rows, hv), BF16)],
        compiler_params=_params("parallel", "arbitrary"),
        name="ret_layer",
    )(x, g.reshape(1, d), w_in, cos, sin, d_in, d_q, d_k, d_c, gn_gain.reshape(1, hv), w_out)


def _qkv_body(x_ref, g_ref, w_ref, q_ref, k_ref, v_ref, kb_ref, vt_ref, km_ref, *, d, blk, scale):
    h = _rms(x_ref[...], g_ref[...]).astype(BF16)
    y = jnp.dot(h, w_ref[...], preferred_element_type=F32)
    k = y[:, d:2 * d]
    v = y[:, 2 * d:]
    q_ref[...] = (y[:, :d] * scale).astype(BF16)
    k_ref[...] = k.reshape(k_ref.shape)
    v_ref[...] = v.reshape(v_ref.shape)
    kb_ref[...] = k.astype(BF16)
    vt_ref[0] = v.T.astype(BF16)
    for r in range(k.shape[0] // blk):
        km_ref[r] = jnp.sum(k[r * blk:(r + 1) * blk], axis=0, keepdims=True) * (1.0 / blk)


def _moba_qkv(x, g, w, seq, *, tm=256):
    n, d = x.shape
    hd = d // MOBA_HEADS
    blk = MOBA_BLOCK
    tm = _row_tile(n, tm)
    assert tm % blk == 0 and seq % tm == 0
    per_seq = seq // tm
    row = lambda i: (i, 0)
    return pl.pallas_call(
        functools.partial(_qkv_body, d=d, blk=blk, scale=LOG2E * hd ** -0.5),
        grid=(n // tm,),
        in_specs=[pl.BlockSpec((tm, d), row), _resident((1, d)), _resident(w.shape)],
        out_specs=[pl.BlockSpec((tm, d), row),
                   pl.BlockSpec((tm, MOBA_HEADS, hd), lambda i: (i, 0, 0)),
                   pl.BlockSpec((tm, MOBA_HEADS, hd), lambda i: (i, 0, 0)),
                   pl.BlockSpec((tm, d), row),
                   pl.BlockSpec((1, d, tm), lambda i: (i // per_seq, 0, i % per_seq)),
                   pl.BlockSpec((tm // blk, 1, d), lambda i: (i, 0, 0))],
        out_shape=[jax.ShapeDtypeStruct((n, d), BF16), jax.ShapeDtypeStruct((n, MOBA_HEADS, hd), F32),
                   jax.ShapeDtypeStruct((n, MOBA_HEADS, hd), F32), jax.ShapeDtypeStruct((n, d), BF16),
                   jax.ShapeDtypeStruct((n // seq, d, seq), BF16),
                   jax.ShapeDtypeStruct((n // blk, 1, d), F32)],
        compiler_params=_params("parallel"),
        name="moba_qkv",
    )(x, g.reshape(1, d), w)


def _qkv_small_body(x_ref, g_ref, w_ref, q_ref, k_ref, v_ref, *, d, scale):
    h = _rms(x_ref[...], g_ref[...]).astype(BF16)
    y = jnp.dot(h, w_ref[...], preferred_element_type=F32)
    q_ref[...] = y[:, :d] * scale
    k_ref[...] = y[:, d:2 * d]
    v_ref[...] = y[:, 2 * d:]


def _moba_qkv_small(x, g, w):
    n, d = x.shape
    hd = d // MOBA_HEADS
    full = pl.BlockSpec((n, d), lambda i: (0, 0))
    return pl.pallas_call(
        functools.partial(_qkv_small_body, d=d, scale=hd ** -0.5),
        grid=(1,),
        in_specs=[full, _resident((1, d)), _resident(w.shape)],
        out_specs=[full, full, full],
        out_shape=[jax.ShapeDtypeStruct((n, d), F32)] * 3,
        compiler_params=_params("arbitrary"),
        name="moba_qkv_small",
    )(x, g.reshape(1, d), w)


def _t5_bucket_np(dist):
    n = np.maximum(dist, 0)
    max_exact = REL_BUCKETS // 2
    nf = np.maximum(n, 1).astype(np.float64)
    large = max_exact + (np.log(nf / max_exact) / math.log(REL_MAX_DIST / max_exact)
                         * (REL_BUCKETS - max_exact)).astype(np.int64)
    large = np.minimum(large, REL_BUCKETS - 1)
    return np.where(n < max_exact, n, large).astype(np.int32)


def _bias_body(bk_ref, tab_ref, o_ref, *, scale):
    bk = bk_ref[0]
    acc = jnp.where(bk < 0, NEG_INF, 0.0).astype(F32)
    for b in range(REL_BUCKETS):
        acc = jnp.where(bk == b, tab_ref[0, b] * scale, acc)
    o_ref[0, 0] = acc


def _bias_tiles(buckets, table, scale=1.0):
    t, r, w = buckets.shape
    g, nbuck, tr, tw = table.shape
    assert nbuck == REL_BUCKETS and tr in (1, r) and tw in (1, w)
    return pl.pallas_call(
        functools.partial(_bias_body, scale=scale),
        grid=(g, t),
        in_specs=[pl.BlockSpec((1, r, w), lambda gi, ti: (ti, 0, 0)),
                  pl.BlockSpec((1, REL_BUCKETS, tr, tw), lambda gi, ti: (gi, 0, 0, 0))],
        out_specs=pl.BlockSpec((1, 1, r, w), lambda gi, ti: (gi, ti, 0, 0)),
        out_shape=jax.ShapeDtypeStruct((g, t, r, w), F32),
        compiler_params=_params("parallel", "parallel"),
        name="t5_bias_tiles",
    )(jnp.asarray(buckets), table)


DEN_ROWS = 16


def _flash_body(q_ref, k_ref, vt_ref, km_ref, bias_ref, w_ref, res_ref, out_ref, o_ref,
                *, nb, blk, hd, topk, nch):
    pair = 2 * hd
    lane = lax.broadcasted_iota(jnp.int32, (1, pair), 1)
    row = lax.broadcasted_iota(jnp.int32, (pair, 1), 0)
    bidx = lax.broadcasted_iota(jnp.int32, (nb, 1), 0)
    nt = (((1,), (1,)), ((), ()))

    wq = 2 * blk

    def lanes(pc):
        return slice(pc * pair, (pc + 1) * pair)

    def both(pc, idx):
        return jnp.concatenate([bias_ref[(2 * pc,) + idx], bias_ref[(2 * pc + 1,) + idx]], axis=1)

    def masked_q(pc):
        qp = q_ref[0, :, lanes(pc)]
        zero = jnp.zeros((blk, pair), BF16)
        return jnp.concatenate([jnp.where(lane < hd, qp, zero), jnp.where(lane >= hd, qp, zero)], axis=0)

    def q_block(t):
        npair = nch // 2
        qs = [masked_q(pc) for pc in range(npair)]
        adds = []
        for pc in range(npair):
            add = [None] * (t + 1)
            add[t] = both(pc, (0,))
            if t >= 1:
                gate = lax.dot_general(km_ref[0, :, lanes(pc)].astype(BF16), qs[pc], nt,
                                       preferred_element_type=F32)
                valid = bidx < t
                gate = jnp.where(valid, gate, NEG_INF)
                rank = jnp.zeros((nb, wq), jnp.int32)
                for j2 in range(t):
                    r2 = gate[j2:j2 + 1, :]
                    ahead = (r2 > gate) | ((r2 == gate) & (j2 < bidx))
                    rank = rank + ahead.astype(jnp.int32)
                sel = valid & (rank < topk)
                add[t - 1] = both(pc, (1,)) + jnp.where(sel[t - 1:t], 0.0, NEG_INF)
                far = jnp.where(sel, both(pc, (2, slice(0, 1))), NEG_INF)
                for j in range(t - 1):
                    add[j] = far[j:j + 1]
            adds.append(add)

        order = [t] + list(range(t))

        def qk(pc, j):
            return lax.dot_general(k_ref[0, j * blk:(j + 1) * blk, lanes(pc)], qs[pc], nt,
                                   preferred_element_type=F32)

        ones = jnp.ones((DEN_ROWS, blk), BF16)

        def pv(pc, j, p):
            vt1 = jnp.concatenate([vt_ref[0, lanes(pc), j * blk:(j + 1) * blk], ones], axis=0)
            return jnp.dot(vt1, p, preferred_element_type=F32)

        state = [dict(x={}, m=None, acc=None) for _ in range(npair)]

        def pass1(pc):
            st = state[pc]
            pend = None
            for j in order:
                s = qk(pc, j)
                yield
                if pend is not None:
                    finish(st, pc, *pend)
                pend = (j, s)
            finish(st, pc, *pend)
            yield

        def finish(st, pc, j, s):
            add = adds[pc][j]
            if add.shape[0] == 1:
                st["x"][j] = s
                mx = s.max(axis=0, keepdims=True) + add
            else:
                st["x"][j] = s + add
                mx = st["x"][j].max(axis=0, keepdims=True)
            st["m"] = mx if st["m"] is None else jnp.maximum(st["m"], mx)

        def pass2(pc):
            st = state[pc]
            pend = None
            for j in order:
                add = adds[pc][j]
                shift = st["m"] - add if add.shape[0] == 1 else st["m"]
                p = jnp.exp2(st["x"][j] - shift).astype(BF16)
                yield
                if pend is not None:
                    a = pv(pc, *pend)
                    st["acc"] = a if st["acc"] is None else st["acc"] + a
                pend = (j, p)
            a = pv(pc, *pend)
            st["acc"] = a if st["acc"] is None else st["acc"] + a
            yield

        def run_together(*gens):
            gens = list(gens)
            while gens:
                for g in list(gens):
                    try:
                        next(g)
                    except StopIteration:
                        gens.remove(g)

        run_together(pass1(0))
        for pc in range(npair):
            if pc + 1 < npair:
                run_together(pass1(pc + 1), pass2(pc))
            else:
                run_together(pass2(pc))
        for pc in range(npair):
            st = state[pc]
            acc = st["acc"][:pair] / st["acc"][pair:pair + 1]
            out_t = jnp.where(row >= hd, acc[:, blk:], acc[:, :blk])
            o_ref[:, lanes(pc)] = out_t.T.astype(BF16)

    for t in range(nb):
        pl.when(pl.program_id(1) == t)(functools.partial(q_block, t))
    out_ref[0] = res_ref[0] + jnp.dot(o_ref[...], w_ref[...], preferred_element_type=F32)


def _moba_flash(q, kb, vt, kmean, bias, w_out, res):
    b, l, d = q.shape
    hd = d // MOBA_HEADS
    assert 2 * hd == LANES and MOBA_HEADS % 2 == 0
    blk = MOBA_BLOCK
    nb = l // blk
    tile = lambda bi, qi: (bi, qi, 0)
    whole = lambda bi, qi: (bi, 0, 0)
    return pl.pallas_call(
        functools.partial(_flash_body, nb=nb, blk=blk, hd=hd, topk=MOBA_TOPK, nch=MOBA_HEADS),
        grid=(b, nb),
        in_specs=[pl.BlockSpec((1, blk, d), tile), pl.BlockSpec((1, l, d), whole),
                  pl.BlockSpec((1, d, l), whole), pl.BlockSpec((1, nb, d), whole),
                  _resident(bias.shape), _resident(w_out.shape), pl.BlockSpec((1, blk, d), tile)],
        out_specs=pl.BlockSpec((1, blk, d), tile),
        out_shape=jax.ShapeDtypeStruct((b, l, d), F32),
        scratch_shapes=[pltpu.VMEM((blk, d), BF16)],
        compiler_params=_params("parallel", "arbitrary"),
        name="moba_flash",
    )(q, kb, vt, kmean, bias, w_out, res)


def _prompt_bias_buckets(blk):
    assert REL_MAX_DIST <= blk + 1
    key = np.arange(blk)[:, None]
    qry = np.arange(blk)[None, :]
    tiles = []
    for dblk in range(3):
        dist = dblk * blk + qry - key
        tiles.append(np.where(dist >= 0, _t5_bucket_np(dist), -1))
    return np.stack(tiles).astype(np.int32)


SAMPLE_BLOCKS_PER_STEP = 8


def _sample_body(pt_ref, bsel_ref, qbdt_ref, *refs, nbp, bps, heads, lq, topk):
    npg = 2 * bps
    k_refs, v_refs = refs[:npg], refs[npg:2 * npg]
    kn_ref, vn_ref, bias_ref, bias_own_ref, o_ref, m_ref, l_ref, g_ref, acc_ref = refs[2 * npg:]
    step = pl.program_id(1)
    hq = heads * lq
    qbdt = qbdt_ref[0]
    d = qbdt.shape[1]
    hd = d // heads
    lane = lax.broadcasted_iota(jnp.int32, (1, LANES), 1)
    nt = (((1,), (1,)), ((), ()))

    @pl.when(step == 0)
    def _():
        m_ref[...] = jnp.zeros(m_ref.shape, F32)
        l_ref[...] = jnp.zeros(l_ref.shape, F32)
        g_ref[...] = jnp.zeros(g_ref.shape, F32)

    def block(k, v, bias, token_major=False):
        kb, vb = k.astype(BF16), v.astype(BF16)
        if token_major:
            sr = lax.dot_general(qbdt, kb, nt, preferred_element_type=F32)
        else:
            sr = jnp.dot(qbdt, kb, preferred_element_type=F32)
        g = jnp.sum(sr, axis=1, keepdims=True) * (1.0 / sr.shape[1])
        sc = sr + bias
        m = jnp.max(sc, axis=1, keepdims=True)
        p = jnp.exp(sc - m)
        l = jnp.sum(p, axis=1, keepdims=True)
        if token_major:
            acc = jnp.dot(p.astype(BF16), vb, preferred_element_type=F32)
        else:
            acc = lax.dot_general(p.astype(BF16), vb, nt, preferred_element_type=F32)
        slab = jnp.concatenate(
            [acc[h * lq:(h + 1) * lq, (h * hd // LANES) * LANES:(h * hd // LANES + 1) * LANES]
             for h in range(heads)], axis=0)
        return m, l, g, slab

    stats = []
    for jj in range(bps):
        j = step * bps + jj
        kt = jnp.concatenate([k_refs[2 * jj][0], k_refs[2 * jj + 1][0]], axis=1)
        vt = jnp.concatenate([v_refs[2 * jj][0], v_refs[2 * jj + 1][0]], axis=1)
        m, l, g, slab = block(kt, vt, bias_ref[bsel_ref[j]])
        acc_ref[j] = slab
        stats.append((j, m, l, g))
    m_all, l_all, g_all = m_ref[...], l_ref[...], g_ref[...]
    for j, m, l, g in stats:
        hit = lane == j
        m_all = jnp.where(hit, m, m_all)
        l_all = jnp.where(hit, l, l_all)
        g_all = jnp.where(hit, g, g_all)
    m_ref[...] = m_all
    l_ref[...] = l_all
    g_ref[...] = g_all

    @pl.when(step == nbp // bps - 1)
    def _():
        zpad = jnp.zeros((PAGE_SIZE - lq, d), F32)
        m, l, _, slab = block(jnp.concatenate([kn_ref[0], zpad], axis=0),
                              jnp.concatenate([vn_ref[0], zpad], axis=0), bias_own_ref[...], token_major=True)
        acc_ref[nbp] = slab
        own = lane == nbp
        m_fin = jnp.where(own, m, m_all)
        l_fin = jnp.where(own, l, l_all)
        lane_f = lane.astype(F32)
        gate = jnp.where(lane < nbp, g_all, NEG_INF)
        sel = jnp.broadcast_to(own, (hq, LANES))
        for _ in range(min(topk, nbp)):
            best = jnp.max(gate, axis=1, keepdims=True)
            first = jnp.min(jnp.where(gate == best, lane_f, float(LANES)), axis=1, keepdims=True)
            pick = lane_f == first
            sel = sel | pick
            gate = jnp.where(pick, NEG_INF, gate)
        top = jnp.max(jnp.where(sel, m_fin, NEG_INF), axis=1, keepdims=True)
        w = jnp.where(sel, jnp.exp(m_fin - top), 0.0)
        wn = w / jnp.sum(w * l_fin, axis=1, keepdims=True)
        out = jnp.zeros((hq, LANES), F32)
        for jj in range(nbp + 1):
            out = out + wn[:, jj:jj + 1] * acc_ref[jj]
        rrow = lax.broadcasted_iota(jnp.int32, (hq, 1), 0)
        keep = (lane // hd) == ((rrow // lq) % (LANES // hd))
        o_ref[0] = jnp.where(keep, out, 0.0)


def _moba_sample(q, k_new, v_new, cache_kt, cache_vt, page_base, page_table, rel_bias):
    b, lq, d = q.shape
    heads = MOBA_HEADS
    hd = d // heads
    hq = heads * lq
    blk = MOBA_BLOCK
    n_pages = page_table.shape[1]
    past = n_pages * PAGE_SIZE
    assert hq == LANES and blk == 2 * PAGE_SIZE and past % blk == 0 and lq <= PAGE_SIZE
    nbp = past // blk
    bps = math.gcd(nbp, SAMPLE_BLOCKS_PER_STEP)
    assert nbp + 1 <= LANES

    same_head = (np.arange(hq)[:, None] // lq) == (np.arange(d)[None, :] // hd)
    qbdt = jnp.where(jnp.asarray(same_head), jnp.tile(q, (1, heads, 1)), 0.0).astype(BF16)

    qry = np.tile(np.arange(lq), heads)[:, None]
    key = np.arange(blk)[None, :]
    tiles, bsel = [], []
    for jb in range(nbp):
        dist = past + qry - (jb * blk + key)
        bk = np.where(dist >= 0, _t5_bucket_np(dist), -1).astype(np.int32)
        for t, existing in enumerate(tiles):
            if np.array_equal(existing, bk):
                bsel.append(t)
                break
        else:
            tiles.append(bk)
            bsel.append(len(tiles) - 1)
    table = jnp.repeat(rel_bias, lq, axis=1)[None, :, :, None]
    bias = _bias_tiles(np.stack(tiles), table)[0]
    dist_own = qry - np.arange(PAGE_SIZE)[None, :]
    own_bk = np.where(dist_own >= 0, _t5_bucket_np(dist_own), -1).astype(np.int32)
    bias_own = _bias_tiles(own_bk[None], table)[0, 0]
    nt = bias.shape[0]

    npg = 2 * bps

    def page(t):
        return lambda bi, s, pt, bs: (page_base + pt[bi * n_pages + s * npg + t], 0, 0)

    per_b = lambda bi, s, pt, bs: (bi, 0, 0)
    page_specs = [pl.BlockSpec((1, d, PAGE_SIZE), page(t)) for t in range(npg)]
    grid_spec = pltpu.PrefetchScalarGridSpec(
        num_scalar_prefetch=2,
        grid=(b, nbp // bps),
        in_specs=[pl.BlockSpec((1, hq, d), per_b)] + page_specs + page_specs + [
            pl.BlockSpec((1, lq, d), per_b), pl.BlockSpec((1, lq, d), per_b),
            pl.BlockSpec((nt, hq, blk), lambda bi, s, pt, bs: (0, 0, 0)),
            pl.BlockSpec((hq, PAGE_SIZE), lambda bi, s, pt, bs: (0, 0))],
        out_specs=pl.BlockSpec((1, hq, LANES), per_b),
        scratch_shapes=[pltpu.VMEM((hq, LANES), F32), pltpu.VMEM((hq, LANES), F32),
                        pltpu.VMEM((hq, LANES), F32), pltpu.VMEM((nbp + 1, hq, LANES), F32)],
    )
    o2 = pl.pallas_call(
        functools.partial(_sample_body, nbp=nbp, bps=bps, heads=heads, lq=lq, topk=MOBA_TOPK),
        grid_spec=grid_spec,
        out_shape=jax.ShapeDtypeStruct((b, hq, LANES), F32),
        compiler_params=_params("parallel", "arbitrary"),
        name="moba_sample",
    )(page_table.reshape(-1), jnp.asarray(np.array(bsel, np.int32)), qbdt,
      *([cache_kt] * npg), *([cache_vt] * npg), k_new, v_new, bias, bias_own)
    o = o2.reshape(b, heads, lq, LANES // hd, hd).sum(axis=3)
    return o.transpose(0, 2, 1, 3).reshape(b, lq, d).astype(BF16)


def kernel(x_prompt, x_sample, state_ret, cache_k, cache_v, page_table, ffn1_norm, ffn1_w_in, ffn1_w_out,
           mix_norm, ffn2_norm, ffn2_w_in, ffn2_w_out, ret_w_in, ret_gn_gain, ret_w_out, moba_w_qkv,
           moba_w_out, rel_bias, final_norm):
    bp, lp, d = x_prompt.shape
    bs, ls, _ = x_sample.shape
    depth = ffn1_norm.shape[0]
    past = page_table.shape[1] * PAGE_SIZE
    pos_p = jnp.arange(lp, dtype=jnp.int32)
    pos_s = past + jnp.arange(ls, dtype=jnp.int32)
    hp = x_prompt.reshape(bp * lp, d)
    hs = x_sample.reshape(bs * ls, d)
    cast = lambda w: w.astype(BF16)
    w1i, w1o, w2i, w2o = cast(ffn1_w_in), cast(ffn1_w_out), cast(ffn2_w_in), cast(ffn2_w_out)
    ret_p, ret_s, kp_l, vp_l, ks_l, vs_l = [], [], [], [], [], []
    for i in range(depth):
        hp = _ffn(hp, ffn1_norm[i], w1i, w1o, i)
        hs = _ffn(hs, ffn1_norm[i], w1i, w1o, i)
        if i % 2 == 0:
            r = i // 2
            w_in, w_out = cast(ret_w_in[r]), cast(ret_w_out[r])
            if lp % RET_CHUNK == 0:
                hp, sp = _ret_layer(hp, mix_norm[i], w_in, pos_p, lp, ret_gn_gain[r], w_out)
            else:
                proj_p = _ret_proj(hp, mix_norm[i], w_in, pos_p, lp).reshape(bp, lp, -1)
                gp, sp = _retention(proj_p, ret_gn_gain[r], None)
                hp = _proj_res(gp.reshape(bp * lp, -1), w_out, hp)
            proj_s = _ret_proj(hs, mix_norm[i], w_in, pos_s, ls).reshape(bs, ls, -1)
            gs, ss = _retention(proj_s, ret_gn_gain[r], state_ret[r])
            hs = _proj_res(gs.reshape(bs * ls, -1), w_out, hs)
            ret_p.append(sp)
            ret_s.append(ss)
        else:
            m = i // 2
            w_qkv, w_out = cast(moba_w_qkv[m]), cast(moba_w_out[m])
            hd = d // MOBA_HEADS
            q, k, v, kb, vt, km = _moba_qkv(hp, mix_norm[i], w_qkv, lp)
            bias = _bias_tiles(_prompt_bias_buckets(MOBA_BLOCK), rel_bias.T[:, :, None, None], LOG2E)
            hp = _moba_flash(q.reshape(bp, lp, d), kb.reshape(bp, lp, d), vt, km.reshape(bp, lp // MOBA_BLOCK, d),
                             bias, w_out, hp.reshape(bp, lp, d)).reshape(bp * lp, d)
            kp_l.append(k.reshape(bp, lp, MOBA_HEADS, hd))
            vp_l.append(v.reshape(bp, lp, MOBA_HEADS, hd))

            qs, ksn, vsn = _moba_qkv_small(hs, mix_norm[i], w_qkv)
            n_pool = cache_k.shape[1]
            pages_t = lambda cache: cache.transpose(0, 1, 3, 4, 2).reshape(-1, d, PAGE_SIZE)
            os_ = _moba_sample(qs.reshape(bs, ls, d), ksn.reshape(bs, ls, d), vsn.reshape(bs, ls, d),
                               pages_t(cache_k), pages_t(cache_v), m * n_pool, page_table, rel_bias)
            hs = _proj_res(os_.reshape(bs * ls, d), w_out, hs)
            ks_l.append(ksn.reshape(bs, ls, MOBA_HEADS, hd))
            vs_l.append(vsn.reshape(bs, ls, MOBA_HEADS, hd))
        last = i == depth - 1
        hp = _ffn(hp, ffn2_norm[i], w2i, w2o, i, final_norm if last else None)
        hs = _ffn(hs, ffn2_norm[i], w2i, w2o, i, final_norm if last else None)
    return (hp.reshape(bp, lp, d), hs.reshape(bs, ls, d), jnp.stack(ret_p), jnp.stack(ret_s),
            jnp.stack(kp_l), jnp.stack(vp_l), jnp.stack(ks_l), jnp.stack(vs_l))
```

```python
import functools
import math

import jax
import jax.numpy as jnp
import numpy as np
from jax import lax
from jax.experimental import pallas as pl
from jax.experimental.pallas import tpu as pltpu

F32 = jnp.float32
BF16 = jnp.bfloat16
NEG_INF = float("-inf")
LOG2E = math.log2(math.e)

LANES = 128
SUBLANES = 8
VMEM_LIMIT_BYTES = 56 * 1024 * 1024

EPS = 1e-6
ROPE_BASE = 10000.0
RET_HEADS = 4
RET_CHUNK = 128
MOBA_HEADS = 16
MOBA_BLOCK = 256
MOBA_TOPK = 3
PAGE_SIZE = 128
REL_BUCKETS = 32
REL_MAX_DIST = 128


def _params(*semantics):
    return pltpu.CompilerParams(dimension_semantics=semantics, vmem_limit_bytes=VMEM_LIMIT_BYTES)


def _resident(shape):
    zeros = (0,) * len(shape)
    return pl.BlockSpec(shape, lambda *_: zeros, pipeline_mode=pl.Buffered(1))


def _resident_layer(stacked, layer):
    idx = (layer,) + (0,) * (stacked.ndim - 1)
    return pl.BlockSpec((None,) + stacked.shape[1:], lambda *_: idx, pipeline_mode=pl.Buffered(1))


def _row_tile(n, want):
    t = min(n, want)
    assert n % t == 0, (n, t)
    return t


def _rms(x, g):
    y = x * lax.rsqrt(jnp.mean(x * x, axis=-1, keepdims=True) + EPS)
    return y * g


def _ffn_body(*refs, d_ff, tf, final):
    if final:
        x_ref, g_ref, win_ref, wout_ref, fg_ref, o_ref = refs
    else:
        x_ref, g_ref, win_ref, wout_ref, o_ref = refs
    x = x_ref[...]
    h = _rms(x, g_ref[...]).astype(BF16)
    y = None
    for c in range(d_ff // tf):
        a = jnp.dot(h, win_ref[:, c * tf:(c + 1) * tf], preferred_element_type=F32)
        b = jnp.dot(h, win_ref[:, d_ff + c * tf:d_ff + (c + 1) * tf], preferred_element_type=F32)
        gated = (a * jax.nn.sigmoid(a) * b).astype(BF16)
        part = jnp.dot(gated, wout_ref[c * tf:(c + 1) * tf, :], preferred_element_type=F32)
        y = part if y is None else y + part
    out = x + 0.5 * y
    if final:
        out = _rms(out, fg_ref[...])
    o_ref[...] = out


MXU_TILE = 256


def _ffn(x, g, w_in, w_out, layer, final_g=None, *, tm=512):
    n, d = x.shape
    d_ff = w_out.shape[1]
    tm = _row_tile(n, tm)
    tf = d_ff if d_ff % MXU_TILE == 0 else LANES
    assert d_ff % tf == 0
    final = final_g is not None
    ins = [x, g.reshape(1, d), w_in, w_out]
    specs = [pl.BlockSpec((tm, d), lambda i: (i, 0)), _resident((1, d)),
             _resident_layer(w_in, layer), _resident_layer(w_out, layer)]
    if final:
        ins.append(final_g.reshape(1, d))
        specs.append(_resident((1, d)))
    return pl.pallas_call(
        functools.partial(_ffn_body, d_ff=d_ff, tf=tf, final=final),
        grid=(n // tm,),
        in_specs=specs,
        out_specs=pl.BlockSpec((tm, d), lambda i: (i, 0)),
        out_shape=jax.ShapeDtypeStruct((n, d), F32),
        compiler_params=_params("parallel"),
        name="ffn",
    )(*ins)


def _ret_proj_body(x_ref, g_ref, w_ref, cos_ref, sin_ref, o_ref, *, nh, dk, dv):
    h = _rms(x_ref[...], g_ref[...]).astype(BF16)
    y = jnp.dot(h, w_ref[...], preferred_element_type=F32)
    cos, sin = cos_ref[...], sin_ref[...]
    half, hk, hv = dk // 2, nh * dk, nh * dv
    for part in range(2):
        for hh in range(nh):
            off = part * hk + hh * dk
            t1, t2 = y[:, off:off + half], y[:, off + half:off + dk]
            r1, r2 = t1 * cos - t2 * sin, t1 * sin + t2 * cos
            if part == 1:
                r1, r2 = r1 * (dk ** -0.5), r2 * (dk ** -0.5)
            o_ref[:, off:off + half] = r1
            o_ref[:, off + half:off + dk] = r2
    o_ref[:, 2 * hk:2 * hk + hv] = y[:, 2 * hk:2 * hk + hv]
    gt = y[:, 2 * hk + hv:]
    o_ref[:, 2 * hk + hv:] = gt * jax.nn.sigmoid(gt)


def _ret_proj(x, g, w, pos, seq, *, tm=256):
    n, d = x.shape
    nout = w.shape[1]
    nh = RET_HEADS
    dk = nout // 6 // nh
    dv = 2 * dk
    half = dk // 2
    tm = _row_tile(n, tm)
    cos, sin = _rope_tables(pos, half)
    if seq >= tm:
        assert seq % tm == 0
        per_seq = seq // tm
        table_map = lambda i: (i % per_seq, 0)
    else:
        assert tm % seq == 0
        cos, sin = jnp.tile(cos, (tm // seq, 1)), jnp.tile(sin, (tm // seq, 1))
        table_map = lambda i: (0, 0)
    return pl.pallas_call(
        functools.partial(_ret_proj_body, nh=nh, dk=dk, dv=dv),
        grid=(n // tm,),
        in_specs=[pl.BlockSpec((tm, d), lambda i: (i, 0)), _resident((1, d)), _resident(w.shape),
                  pl.BlockSpec((tm, half), table_map), pl.BlockSpec((tm, half), table_map)],
        out_specs=pl.BlockSpec((tm, nout), lambda i: (i, 0)),
        out_shape=jax.ShapeDtypeStruct((n, nout), F32),
        compiler_params=_params("parallel"),
        name="ret_proj",
    )(x, g.reshape(1, d), w, cos, sin)


def _proj_res_body(x_ref, w_ref, r_ref, o_ref):
    o_ref[...] = r_ref[...] + jnp.dot(x_ref[...], w_ref[...], preferred_element_type=F32)


def _proj_res(x, w, res, *, tm=512):
    n, k = x.shape
    d = w.shape[1]
    tm = _row_tile(n, tm)
    return pl.pallas_call(
        _proj_res_body,
        grid=(n // tm,),
        in_specs=[pl.BlockSpec((tm, k), lambda i: (i, 0)), _resident(w.shape),
                  pl.BlockSpec((tm, d), lambda i: (i, 0))],
        out_specs=pl.BlockSpec((tm, d), lambda i: (i, 0)),
        out_shape=jax.ShapeDtypeStruct((n, d), F32),
        compiler_params=_params("parallel"),
        name="proj_res",
    )(x, w, res)


def _ret_body(*refs, nh, dk, dv, c, cp, has_s0):
    if has_s0:
        (q_ref, k_ref, v_ref, g_ref, din_ref, dq_ref, dkk_ref, dc_ref, gain_ref, s0_ref, o_ref, s_ref) = refs
    else:
        (q_ref, k_ref, v_ref, g_ref, din_ref, dq_ref, dkk_ref, dc_ref, gain_ref, o_ref, s_ref) = refs
    i = pl.program_id(1)

    @pl.when(i == 0)
    def _():
        if has_s0:
            s_ref[...] = s0_ref[...]
        else:
            s_ref[...] = jnp.zeros(s_ref.shape, F32)

    def pad(t):
        if cp == c:
            return t
        return jnp.concatenate([t, jnp.zeros((cp - c, t.shape[1]), t.dtype)], axis=0)

    heads = range(nh)
    nt = (((1,), (1,)), ((), ()))
    kr = [pad(k_ref[0, :, h * dk:(h + 1) * dk]) for h in heads]
    qb = [pad(q_ref[0, :, h * dk:(h + 1) * dk]).astype(BF16) for h in heads]
    vb = [pad(v_ref[0, :, h * dv:(h + 1) * dv]).astype(BF16) for h in heads]
    att = [lax.dot_general(qb[h], kr[h].astype(BF16), nt, preferred_element_type=F32) for h in heads]
    o_x = [jnp.dot(qb[h], s_ref[0, h].astype(BF16), preferred_element_type=F32) for h in heads]
    o_in = [jnp.dot((att[h] * din_ref[h]).astype(BF16), vb[h], preferred_element_type=F32) for h in heads]
    upd = [jnp.dot((kr[h] * dkk_ref[h]).T.astype(BF16), vb[h], preferred_element_type=F32) for h in heads]
    for h in heads:
        s_ref[0, h] = s_ref[0, h] * dc_ref[h] + upd[h]
    for h in heads:
        o = (o_in[h] + o_x[h] * dq_ref[h])[:c]
        mu = jnp.mean(o, axis=-1, keepdims=True)
        d = o - mu
        var = jnp.mean(d * d, axis=-1, keepdims=True)
        on = d * lax.rsqrt(var + EPS) * gain_ref[:, h * dv:(h + 1) * dv]
        o_ref[0, :, h * dv:(h + 1) * dv] = (g_ref[0, :, h * dv:(h + 1) * dv] * on).astype(BF16)


def _decay_tables(nh, c, cp):
    log_g = jnp.log1p(-jnp.exp2(-5.0 - jnp.arange(nh, dtype=F32)))
    idx = jnp.arange(c, dtype=F32)
    rel = idx[:, None] - idx[None, :]
    d_in = jnp.where(rel[None] >= 0, jnp.exp(log_g[:, None, None] * jnp.maximum(rel, 0.0)[None]), 0.0)
    d_q = jnp.exp(log_g[:, None] * (idx[None, :] + 1.0))
    d_k = jnp.exp(log_g[:, None] * (c - 1.0 - idx[None, :]))
    d_c = jnp.exp(log_g * c)
    d_in = jnp.pad(d_in, ((0, 0), (0, cp - c), (0, cp - c)))
    d_q = jnp.pad(d_q, ((0, 0), (0, cp - c)))[:, :, None]
    d_k = jnp.pad(d_k, ((0, 0), (0, cp - c)))[:, :, None]
    return d_in, d_q, d_k, d_c[:, None, None]


def _rope_tables(pos, half):
    inv = ROPE_BASE ** (-jnp.arange(half, dtype=F32) / half)
    ang = pos.astype(F32)[:, None] * inv[None, :]
    return jnp.cos(ang), jnp.sin(ang)


def _retention(proj, gn_gain, s0):
    b, l, width = proj.shape
    nh = RET_HEADS
    hk = width // 6
    hv = 2 * hk
    dk, dv = hk // nh, hv // nh
    c = math.gcd(l, RET_CHUNK)
    cp = RET_CHUNK
    n = l // c
    d_in, d_q, d_k, d_c = _decay_tables(nh, c, cp)

    has_s0 = s0 is not None
    ins = [proj, proj, proj, proj, d_in, d_q, d_k, d_c, gn_gain.reshape(1, hv)]
    specs = [
        pl.BlockSpec((1, c, hk), lambda bi, i: (bi, i, 0)),
        pl.BlockSpec((1, c, hk), lambda bi, i: (bi, i, 1)),
        pl.BlockSpec((1, c, hv), lambda bi, i: (bi, i, 1)),
        pl.BlockSpec((1, c, hv), lambda bi, i: (bi, i, 2)),
        _resident(d_in.shape), _resident(d_q.shape), _resident(d_k.shape), _resident(d_c.shape),
        _resident((1, hv)),
    ]
    if has_s0:
        ins.append(s0)
        specs.append(pl.BlockSpec((1, nh, dk, dv), lambda bi, i: (bi, 0, 0, 0)))
    return pl.pallas_call(
        functools.partial(_ret_body, nh=nh, dk=dk, dv=dv, c=c, cp=cp, has_s0=has_s0),
        grid=(b, n),
        in_specs=specs,
        out_specs=[pl.BlockSpec((1, c, hv), lambda bi, i: (bi, i, 0)),
                   pl.BlockSpec((1, nh, dk, dv), lambda bi, i: (bi, 0, 0, 0))],
        out_shape=[jax.ShapeDtypeStruct((b, l, hv), BF16),
                   jax.ShapeDtypeStruct((b, nh, dk, dv), F32)],
        compiler_params=_params("parallel", "arbitrary"),
        name="retention",
    )(*ins)


def _ret_layer_body(x_ref, g_ref, win_ref, cos_ref, sin_ref, din_ref, dq_ref, dkk_ref, dc_ref, gain_ref,
                    wout_ref, o_ref, s_ref, gated_ref, *, nh, dk, dv, c):
    @pl.when(pl.program_id(1) == 0)
    def _():
        s_ref[...] = jnp.zeros(s_ref.shape, F32)

    x = x_ref[...]
    y = jnp.dot(_rms(x, g_ref[...]).astype(BF16), win_ref[...], preferred_element_type=F32)
    cos, sin = cos_ref[...], sin_ref[...]
    half, hk, hv = dk // 2, nh * dk, nh * dv

    def rope(off, scale=None):
        t1, t2 = y[:, off:off + half], y[:, off + half:off + dk]
        r1, r2 = t1 * cos - t2 * sin, t1 * sin + t2 * cos
        if scale is not None:
            r1, r2 = r1 * scale, r2 * scale
        return jnp.concatenate([r1, r2], axis=-1)

    heads = range(nh)
    nt = (((1,), (1,)), ((), ()))
    q = [rope(h * dk).astype(BF16) for h in heads]
    k = [rope(hk + h * dk, dk ** -0.5) for h in heads]
    v = [y[:, 2 * hk + h * dv:2 * hk + (h + 1) * dv].astype(BF16) for h in heads]
    for ck in range(x.shape[0] // c):
        r = slice(ck * c, (ck + 1) * c)
        att = [lax.dot_general(q[h][r], k[h][r].astype(BF16), nt, preferred_element_type=F32) for h in heads]
        o_x = [jnp.dot(q[h][r], s_ref[0, h].astype(BF16), preferred_element_type=F32) for h in heads]
        o_in = [jnp.dot((att[h] * din_ref[h]).astype(BF16), v[h][r], preferred_element_type=F32) for h in heads]
        upd = [jnp.dot((k[h][r] * dkk_ref[h]).T.astype(BF16), v[h][r], preferred_element_type=F32)
               for h in heads]
        for h in heads:
            s_ref[0, h] = s_ref[0, h] * dc_ref[h] + upd[h]
        for h in heads:
            o = o_in[h] + o_x[h] * dq_ref[h]
            mu = jnp.mean(o, axis=-1, keepdims=True)
            d = o - mu
            var = jnp.mean(d * d, axis=-1, keepdims=True)
            on = d * lax.rsqrt(var + EPS) * gain_ref[:, h * dv:(h + 1) * dv]
            gt = y[r, 2 * hk + hv + h * dv:2 * hk + hv + (h + 1) * dv]
            gated_ref[r, h * dv:(h + 1) * dv] = (gt * jax.nn.sigmoid(gt) * on).astype(BF16)
    o_ref[...] = x + jnp.dot(gated_ref[...], wout_ref[...], preferred_element_type=F32)


def _ret_layer(x, g, w_in, pos, seq, gn_gain, w_out, *, rows=512):
    n, d = x.shape
    nh = RET_HEADS
    hk = w_in.shape[1] // 6
    hv = 2 * hk
    dk, dv = hk // nh, hv // nh
    c = RET_CHUNK
    rows = min(rows, seq)
    assert seq % rows == 0 and rows % c == 0
    per_seq = seq // rows
    cos, sin = _rope_tables(pos, dk // 2)
    d_in, d_q, d_k, d_c = _decay_tables(nh, c, c)
    x_map = lambda bi, i: (bi * per_seq + i, 0)
    tab_map = lambda bi, i: (i, 0)
    return pl.pallas_call(
        functools.partial(_ret_layer_body, nh=nh, dk=dk, dv=dv, c=c),
        grid=(n // seq, per_seq),
        in_specs=[pl.BlockSpec((rows, d), x_map), _resident((1, d)), _resident(w_in.shape),
                  pl.BlockSpec((rows, dk // 2), tab_map), pl.BlockSpec((rows, dk // 2), tab_map),
                  _resident(d_in.shape), _resident(d_q.shape), _resident(d_k.shape), _resident(d_c.shape),
                  _resident((1, hv)), _resident(w_out.shape)],
        out_specs=[pl.BlockSpec((rows, d), x_map),
                   pl.BlockSpec((1, nh, dk, dv), lambda bi, i: (bi, 0, 0, 0))],
        out_shape=[jax.ShapeDtypeStruct((n, d), F32), jax.ShapeDtypeStruct((n // seq, nh, dk, dv), F32)],
        scratch_shapes=[pltpu.VMEM((rows, hv), BF16)],
        compiler_params=_params("parallel", "arbitrary"),
        name="ret_layer",
    )(x, g.reshape(1, d), w_in, cos, sin, d_in, d_q, d_k, d_c, gn_gain.reshape(1, hv), w_out)


def _qkv_body(x_ref, g_ref, w_ref, q_ref, k_ref, v_ref, kb_ref, vt_ref, km_ref, *, d, blk, scale):
    h = _rms(x_ref[...], g_ref[...]).astype(BF16)
    y = jnp.dot(h, w_ref[...], preferred_element_type=F32)
    k = y[:, d:2 * d]
    v = y[:, 2 * d:]
    q_ref[...] = (y[:, :d] * scale).astype(BF16)
    k_ref[...] = k.reshape(k_ref.shape)
    v_ref[...] = v.reshape(v_ref.shape)
    kb_ref[...] = k.astype(BF16)
    vt_ref[0] = v.T.astype(BF16)
    for r in range(k.shape[0] // blk):
        km_ref[r] = jnp.sum(k[r * blk:(r + 1) * blk], axis=0, keepdims=True) * (1.0 / blk)


def _moba_qkv(x, g, w, seq, *, tm=256):
    n, d = x.shape
    hd = d // MOBA_HEADS
    blk = MOBA_BLOCK
    tm = _row_tile(n, tm)
    assert tm % blk == 0 and seq % tm == 0
    per_seq = seq // tm
    row = lambda i: (i, 0)
    return pl.pallas_call(
        functools.partial(_qkv_body, d=d, blk=blk, scale=LOG2E * hd ** -0.5),
        grid=(n // tm,),
        in_specs=[pl.BlockSpec((tm, d), row), _resident((1, d)), _resident(w.shape)],
        out_specs=[pl.BlockSpec((tm, d), row),
                   pl.BlockSpec((tm, MOBA_HEADS, hd), lambda i: (i, 0, 0)),
                   pl.BlockSpec((tm, MOBA_HEADS, hd), lambda i: (i, 0, 0)),
                   pl.BlockSpec((tm, d), row),
                   pl.BlockSpec((1, d, tm), lambda i: (i // per_seq, 0, i % per_seq)),
                   pl.BlockSpec((tm // blk, 1, d), lambda i: (i, 0, 0))],
        out_shape=[jax.ShapeDtypeStruct((n, d), BF16), jax.ShapeDtypeStruct((n, MOBA_HEADS, hd), F32),
                   jax.ShapeDtypeStruct((n, MOBA_HEADS, hd), F32), jax.ShapeDtypeStruct((n, d), BF16),
                   jax.ShapeDtypeStruct((n // seq, d, seq), BF16),
                   jax.ShapeDtypeStruct((n // blk, 1, d), F32)],
        compiler_params=_params("parallel"),
        name="moba_qkv",
    )(x, g.reshape(1, d), w)


def _qkv_small_body(x_ref, g_ref, w_ref, q_ref, k_ref, v_ref, *, d, scale):
    h = _rms(x_ref[...], g_ref[...]).astype(BF16)
    y = jnp.dot(h, w_ref[...], preferred_element_type=F32)
    q_ref[...] = y[:, :d] * scale
    k_ref[...] = y[:, d:2 * d]
    v_ref[...] = y[:, 2 * d:]


def _moba_qkv_small(x, g, w):
    n, d = x.shape
    hd = d // MOBA_HEADS
    full = pl.BlockSpec((n, d), lambda i: (0, 0))
    return pl.pallas_call(
        functools.partial(_qkv_small_body, d=d, scale=hd ** -0.5),
        grid=(1,),
        in_specs=[full, _resident((1, d)), _resident(w.shape)],
        out_specs=[full, full, full],
        out_shape=[jax.ShapeDtypeStruct((n, d), F32)] * 3,
        compiler_params=_params("arbitrary"),
        name="moba_qkv_small",
    )(x, g.reshape(1, d), w)


def _t5_bucket_np(dist):
    n = np.maximum(dist, 0)
    max_exact = REL_BUCKETS // 2
    nf = np.maximum(n, 1).astype(np.float64)
    large = max_exact + (np.log(nf / max_exact) / math.log(REL_MAX_DIST / max_exact)
                         * (REL_BUCKETS - max_exact)).astype(np.int64)
    large = np.minimum(large, REL_BUCKETS - 1)
    return np.where(n < max_exact, n, large).astype(np.int32)


def _bias_body(bk_ref, tab_ref, o_ref, *, scale):
    bk = bk_ref[0]
    acc = jnp.where(bk < 0, NEG_INF, 0.0).astype(F32)
    for b in range(REL_BUCKETS):
        acc = jnp.where(bk == b, tab_ref[0, b] * scale, acc)
    o_ref[0, 0] = acc


def _bias_tiles(buckets, table, scale=1.0):
    t, r, w = buckets.shape
    g, nbuck, tr, tw = table.shape
    assert nbuck == REL_BUCKETS and tr in (1, r) and tw in (1, w)
    return pl.pallas_call(
        functools.partial(_bias_body, scale=scale),
        grid=(g, t),
        in_specs=[pl.BlockSpec((1, r, w), lambda gi, ti: (ti, 0, 0)),
                  pl.BlockSpec((1, REL_BUCKETS, tr, tw), lambda gi, ti: (gi, 0, 0, 0))],
        out_specs=pl.BlockSpec((1, 1, r, w), lambda gi, ti: (gi, ti, 0, 0)),
        out_shape=jax.ShapeDtypeStruct((g, t, r, w), F32),
        compiler_params=_params("parallel", "parallel"),
        name="t5_bias_tiles",
    )(jnp.asarray(buckets), table)


FLASH_HEADS = 8
DEN_ROWS = 16


def _flash_body(q_ref, k_ref, vt_ref, km_ref, bias_ref, o_ref, *, nb, blk, hd, topk, nch):
    pair = 2 * hd
    lane = lax.broadcasted_iota(jnp.int32, (1, pair), 1)
    row = lax.broadcasted_iota(jnp.int32, (pair, 1), 0)
    bidx = lax.broadcasted_iota(jnp.int32, (nb, 1), 0)
    nt = (((1,), (1,)), ((), ()))

    wq = 2 * blk

    def lanes(pc):
        return slice(pc * pair, (pc + 1) * pair)

    def both(pc, idx):
        return jnp.concatenate([bias_ref[(2 * pc,) + idx], bias_ref[(2 * pc + 1,) + idx]], axis=1)

    def masked_q(pc):
        qp = q_ref[0, :, lanes(pc)]
        zero = jnp.zeros((blk, pair), BF16)
        return jnp.concatenate([jnp.where(lane < hd, qp, zero), jnp.where(lane >= hd, qp, zero)], axis=0)

    def q_block(t):
        npair = nch // 2
        qs = [masked_q(pc) for pc in range(npair)]
        adds = []
        for pc in range(npair):
            add = [None] * (t + 1)
            add[t] = both(pc, (0,))
            if t >= 1:
                gate = lax.dot_general(km_ref[0, :, lanes(pc)].astype(BF16), qs[pc], nt,
                                       preferred_element_type=F32)
                valid = bidx < t
                gate = jnp.where(valid, gate, NEG_INF)
                rank = jnp.zeros((nb, wq), jnp.int32)
                for j2 in range(t):
                    r2 = gate[j2:j2 + 1, :]
                    ahead = (r2 > gate) | ((r2 == gate) & (j2 < bidx))
                    rank = rank + ahead.astype(jnp.int32)
                sel = valid & (rank < topk)
                add[t - 1] = both(pc, (1,)) + jnp.where(sel[t - 1:t], 0.0, NEG_INF)
                far = jnp.where(sel, both(pc, (2, slice(0, 1))), NEG_INF)
                for j in range(t - 1):
                    add[j] = far[j:j + 1]
            adds.append(add)

        order = [t] + list(range(t))

        def qk(pc, j):
            return lax.dot_general(k_ref[0, j * blk:(j + 1) * blk, lanes(pc)], qs[pc], nt,
                                   preferred_element_type=F32)

        ones = jnp.ones((DEN_ROWS, blk), BF16)

        def pv(pc, j, p):
            vt1 = jnp.concatenate([vt_ref[0, lanes(pc), j * blk:(j + 1) * blk], ones], axis=0)
            return jnp.dot(vt1, p, preferred_element_type=F32)

        state = [dict(x={}, m=None, acc=None) for _ in range(npair)]

        def pass1(pc):
            st = state[pc]
            pend = None
            for j in order:
                s = qk(pc, j)
                yield
                if pend is not None:
                    finish(st, pc, *pend)
                pend = (j, s)
            finish(st, pc, *pend)
            yield

        def finish(st, pc, j, s):
            add = adds[pc][j]
            if add.shape[0] == 1:
                st["x"][j] = s
                mx = s.max(axis=0, keepdims=True) + add
            else:
                st["x"][j] = s + add
                mx = st["x"][j].max(axis=0, keepdims=True)
            st["m"] = mx if st["m"] is None else jnp.maximum(st["m"], mx)

        def pass2(pc):
            st = state[pc]
            pend = None
            for j in order:
                add = adds[pc][j]
                shift = st["m"] - add if add.shape[0] == 1 else st["m"]
                p = jnp.exp2(st["x"][j] - shift).astype(BF16)
                yield
                if pend is not None:
                    a = pv(pc, *pend)
                    st["acc"] = a if st["acc"] is None else st["acc"] + a
                pend = (j, p)
            a = pv(pc, *pend)
            st["acc"] = a if st["acc"] is None else st["acc"] + a
            yield

        def run_together(*gens):
            gens = list(gens)
            while gens:
                for g in list(gens):
                    try:
                        next(g)
                    except StopIteration:
                        gens.remove(g)

        run_together(pass1(0))
        for pc in range(npair):
            if pc + 1 < npair:
                run_together(pass1(pc + 1), pass2(pc))
            else:
                run_together(pass2(pc))
        for pc in range(npair):
            st = state[pc]
            acc = st["acc"][:pair] / st["acc"][pair:pair + 1]
            out_t = jnp.where(row >= hd, acc[:, blk:], acc[:, :blk])
            o_ref[0, :, lanes(pc)] = out_t.T.astype(BF16)

    for t in range(nb):
        pl.when(pl.program_id(2) == t)(functools.partial(q_block, t))


def _moba_flash(q, kb, vt, kmean, bias):
    b, l, d = q.shape
    hd = d // MOBA_HEADS
    pair = 2 * hd
    assert pair == LANES
    blk = MOBA_BLOCK
    nb = l // blk
    nch = FLASH_HEADS
    w = nch * hd
    assert nch % 2 == 0 and d % w == 0
    return pl.pallas_call(
        functools.partial(_flash_body, nb=nb, blk=blk, hd=hd, topk=MOBA_TOPK, nch=nch),
        grid=(b, d // w, nb),
        in_specs=[pl.BlockSpec((1, blk, w), lambda bi, p, qi: (bi, qi, p)),
                  pl.BlockSpec((1, l, w), lambda bi, p, qi: (bi, 0, p)),
                  pl.BlockSpec((1, w, l), lambda bi, p, qi: (bi, p, 0)),
                  pl.BlockSpec((1, nb, w), lambda bi, p, qi: (bi, 0, p)),
                  _resident(bias.shape) if d == w else
                  pl.BlockSpec((nch, 3, blk, blk), lambda bi, p, qi: (p, 0, 0, 0))],
        out_specs=pl.BlockSpec((1, blk, w), lambda bi, p, qi: (bi, qi, p)),
        out_shape=jax.ShapeDtypeStruct((b, l, d), BF16),
        compiler_params=_params("parallel", "parallel", "arbitrary"),
        name="moba_flash",
    )(q, kb, vt, kmean, bias)


def _prompt_bias_buckets(blk):
    assert REL_MAX_DIST <= blk + 1
    key = np.arange(blk)[:, None]
    qry = np.arange(blk)[None, :]
    tiles = []
    for dblk in range(3):
        dist = dblk * blk + qry - key
        tiles.append(np.where(dist >= 0, _t5_bucket_np(dist), -1))
    return np.stack(tiles).astype(np.int32)


SAMPLE_BLOCKS_PER_STEP = 8


def _sample_body(pt_ref, bsel_ref, qbdt_ref, *refs, nbp, bps, heads, lq, topk):
    npg = 2 * bps
    k_refs, v_refs = refs[:npg], refs[npg:2 * npg]
    kn_ref, vn_ref, bias_ref, bias_own_ref, o_ref, m_ref, l_ref, g_ref, acc_ref = refs[2 * npg:]
    step = pl.program_id(1)
    hq = heads * lq
    qbdt = qbdt_ref[0]
    d = qbdt.shape[1]
    hd = d // heads
    lane = lax.broadcasted_iota(jnp.int32, (1, LANES), 1)
    nt = (((1,), (1,)), ((), ()))

    @pl.when(step == 0)
    def _():
        m_ref[...] = jnp.zeros(m_ref.shape, F32)
        l_ref[...] = jnp.zeros(l_ref.shape, F32)
        g_ref[...] = jnp.zeros(g_ref.shape, F32)

    def block(k, v, bias, token_major=False):
        kb, vb = k.astype(BF16), v.astype(BF16)
        if token_major:
            sr = lax.dot_general(qbdt, kb, nt, preferred_element_type=F32)
        else:
            sr = jnp.dot(qbdt, kb, preferred_element_type=F32)
        g = jnp.sum(sr, axis=1, keepdims=True) * (1.0 / sr.shape[1])
        sc = sr + bias
        m = jnp.max(sc, axis=1, keepdims=True)
        p = jnp.exp(sc - m)
        l = jnp.sum(p, axis=1, keepdims=True)
        if token_major:
            acc = jnp.dot(p.astype(BF16), vb, preferred_element_type=F32)
        else:
            acc = lax.dot_general(p.astype(BF16), vb, nt, preferred_element_type=F32)
        slab = jnp.concatenate(
            [acc[h * lq:(h + 1) * lq, (h * hd // LANES) * LANES:(h * hd // LANES + 1) * LANES]
             for h in range(heads)], axis=0)
        return m, l, g, slab

    stats = []
    for jj in range(bps):
        j = step * bps + jj
        kt = jnp.concatenate([k_refs[2 * jj][0], k_refs[2 * jj + 1][0]], axis=1)
        vt = jnp.concatenate([v_refs[2 * jj][0], v_refs[2 * jj + 1][0]], axis=1)
        m, l, g, slab = block(kt, vt, bias_ref[bsel_ref[j]])
        acc_ref[j] = slab
        stats.append((j, m, l, g))
    m_all, l_all, g_all = m_ref[...], l_ref[...], g_ref[...]
    for j, m, l, g in stats:
        hit = lane == j
        m_all = jnp.where(hit, m, m_all)
        l_all = jnp.where(hit, l, l_all)
        g_all = jnp.where(hit, g, g_all)
    m_ref[...] = m_all
    l_ref[...] = l_all
    g_ref[...] = g_all

    @pl.when(step == nbp // bps - 1)
    def _():
        zpad = jnp.zeros((PAGE_SIZE - lq, d), F32)
        m, l, _, slab = block(jnp.concatenate([kn_ref[0], zpad], axis=0),
                              jnp.concatenate([vn_ref[0], zpad], axis=0), bias_own_ref[...], token_major=True)
        acc_ref[nbp] = slab
        own = lane == nbp
        m_fin = jnp.where(own, m, m_all)
        l_fin = jnp.where(own, l, l_all)
        lane_f = lane.astype(F32)
        gate = jnp.where(lane < nbp, g_all, NEG_INF)
        sel = jnp.broadcast_to(own, (hq, LANES))
        for _ in range(min(topk, nbp)):
            best = jnp.max(gate, axis=1, keepdims=True)
            first = jnp.min(jnp.where(gate == best, lane_f, float(LANES)), axis=1, keepdims=True)
            pick = lane_f == first
            sel = sel | pick
            gate = jnp.where(pick, NEG_INF, gate)
        top = jnp.max(jnp.where(sel, m_fin, NEG_INF), axis=1, keepdims=True)
        w = jnp.where(sel, jnp.exp(m_fin - top), 0.0)
        wn = w / jnp.sum(w * l_fin, axis=1, keepdims=True)
        out = jnp.zeros((hq, LANES), F32)
        for jj in range(nbp + 1):
            out = out + wn[:, jj:jj + 1] * acc_ref[jj]
        rrow = lax.broadcasted_iota(jnp.int32, (hq, 1), 0)
        keep = (lane // hd) == ((rrow // lq) % (LANES // hd))
        o_ref[0] = jnp.where(keep, out, 0.0)


def _moba_sample(q, k_new, v_new, cache_kt, cache_vt, page_base, page_table, rel_bias):
    b, lq, d = q.shape
    heads = MOBA_HEADS
    hd = d // heads
    hq = heads * lq
    blk = MOBA_BLOCK
    n_pages = page_table.shape[1]
    past = n_pages * PAGE_SIZE
    assert hq == LANES and blk == 2 * PAGE_SIZE and past % blk == 0 and lq <= PAGE_SIZE
    nbp = past // blk
    bps = math.gcd(nbp, SAMPLE_BLOCKS_PER_STEP)
    assert nbp + 1 <= LANES

    same_head = (np.arange(hq)[:, None] // lq) == (np.arange(d)[None, :] // hd)
    qbdt = jnp.where(jnp.asarray(same_head), jnp.tile(q, (1, heads, 1)), 0.0).astype(BF16)

    qry = np.tile(np.arange(lq), heads)[:, None]
    key = np.arange(blk)[None, :]
    tiles, bsel = [], []
    for jb in range(nbp):
        dist = past + qry - (jb * blk + key)
        bk = np.where(dist >= 0, _t5_bucket_np(dist), -1).astype(np.int32)
        for t, existing in enumerate(tiles):
            if np.array_equal(existing, bk):
                bsel.append(t)
                break
        else:
            tiles.append(bk)
            bsel.append(len(tiles) - 1)
    table = jnp.repeat(rel_bias, lq, axis=1)[None, :, :, None]
    bias = _bias_tiles(np.stack(tiles), table)[0]
    dist_own = qry - np.arange(PAGE_SIZE)[None, :]
    own_bk = np.where(dist_own >= 0, _t5_bucket_np(dist_own), -1).astype(np.int32)
    bias_own = _bias_tiles(own_bk[None], table)[0, 0]
    nt = bias.shape[0]

    npg = 2 * bps

    def page(t):
        return lambda bi, s, pt, bs: (page_base + pt[bi * n_pages + s * npg + t], 0, 0)

    per_b = lambda bi, s, pt, bs: (bi, 0, 0)
    page_specs = [pl.BlockSpec((1, d, PAGE_SIZE), page(t)) for t in range(npg)]
    grid_spec = pltpu.PrefetchScalarGridSpec(
        num_scalar_prefetch=2,
        grid=(b, nbp // bps),
        in_specs=[pl.BlockSpec((1, hq, d), per_b)] + page_specs + page_specs + [
            pl.BlockSpec((1, lq, d), per_b), pl.BlockSpec((1, lq, d), per_b),
            pl.BlockSpec((nt, hq, blk), lambda bi, s, pt, bs: (0, 0, 0)),
            pl.BlockSpec((hq, PAGE_SIZE), lambda bi, s, pt, bs: (0, 0))],
        out_specs=pl.BlockSpec((1, hq, LANES), per_b),
        scratch_shapes=[pltpu.VMEM((hq, LANES), F32), pltpu.VMEM((hq, LANES), F32),
                        pltpu.VMEM((hq, LANES), F32), pltpu.VMEM((nbp + 1, hq, LANES), F32)],
    )
    o2 = pl.pallas_call(
        functools.partial(_sample_body, nbp=nbp, bps=bps, heads=heads, lq=lq, topk=MOBA_TOPK),
        grid_spec=grid_spec,
        out_shape=jax.ShapeDtypeStruct((b, hq, LANES), F32),
        compiler_params=_params("parallel", "arbitrary"),
        name="moba_sample",
    )(page_table.reshape(-1), jnp.asarray(np.array(bsel, np.int32)), qbdt,
      *([cache_kt] * npg), *([cache_vt] * npg), k_new, v_new, bias, bias_own)
    o = o2.reshape(b, heads, lq, LANES // hd, hd).sum(axis=3)
    return o.transpose(0, 2, 1, 3).reshape(b, lq, d).astype(BF16)


def kernel(x_prompt, x_sample, state_ret, cache_k, cache_v, page_table, ffn1_norm, ffn1_w_in, ffn1_w_out,
           mix_norm, ffn2_norm, ffn2_w_in, ffn2_w_out, ret_w_in, ret_gn_gain, ret_w_out, moba_w_qkv,
           moba_w_out, rel_bias, final_norm):
    bp, lp, d = x_prompt.shape
    bs, ls, _ = x_sample.shape
    depth = ffn1_norm.shape[0]
    past = page_table.shape[1] * PAGE_SIZE
    pos_p = jnp.arange(lp, dtype=jnp.int32)
    pos_s = past + jnp.arange(ls, dtype=jnp.int32)
    hp = x_prompt.reshape(bp * lp, d)
    hs = x_sample.reshape(bs * ls, d)
    cast = lambda w: w.astype(BF16)
    w1i, w1o, w2i, w2o = cast(ffn1_w_in), cast(ffn1_w_out), cast(ffn2_w_in), cast(ffn2_w_out)
    ret_p, ret_s, kp_l, vp_l, ks_l, vs_l = [], [], [], [], [], []
    for i in range(depth):
        hp = _ffn(hp, ffn1_norm[i], w1i, w1o, i)
        hs = _ffn(hs, ffn1_norm[i], w1i, w1o, i)
        if i % 2 == 0:
            r = i // 2
            w_in, w_out = cast(ret_w_in[r]), cast(ret_w_out[r])
            if lp % RET_CHUNK == 0:
                hp, sp = _ret_layer(hp, mix_norm[i], w_in, pos_p, lp, ret_gn_gain[r], w_out)
            else:
                proj_p = _ret_proj(hp, mix_norm[i], w_in, pos_p, lp).reshape(bp, lp, -1)
                gp, sp = _retention(proj_p, ret_gn_gain[r], None)
                hp = _proj_res(gp.reshape(bp * lp, -1), w_out, hp)
            proj_s = _ret_proj(hs, mix_norm[i], w_in, pos_s, ls).reshape(bs, ls, -1)
            gs, ss = _retention(proj_s, ret_gn_gain[r], state_ret[r])
            hs = _proj_res(gs.reshape(bs * ls, -1), w_out, hs)
            ret_p.append(sp)
            ret_s.append(ss)
        else:
            m = i // 2
            w_qkv, w_out = cast(moba_w_qkv[m]), cast(moba_w_out[m])
            hd = d // MOBA_HEADS
            q, k, v, kb, vt, km = _moba_qkv(hp, mix_norm[i], w_qkv, lp)
            bias = _bias_tiles(_prompt_bias_buckets(MOBA_BLOCK), rel_bias.T[:, :, None, None], LOG2E)
            op = _moba_flash(q.reshape(bp, lp, d), kb.reshape(bp, lp, d), vt,
                             km.reshape(bp, lp // MOBA_BLOCK, d), bias)
            hp = _proj_res(op.reshape(bp * lp, d), w_out, hp)
            kp_l.append(k.reshape(bp, lp, MOBA_HEADS, hd))
            vp_l.append(v.reshape(bp, lp, MOBA_HEADS, hd))

            qs, ksn, vsn = _moba_qkv_small(hs, mix_norm[i], w_qkv)
            n_pool = cache_k.shape[1]
            pages_t = lambda cache: cache.transpose(0, 1, 3, 4, 2).reshape(-1, d, PAGE_SIZE)
            os_ = _moba_sample(qs.reshape(bs, ls, d), ksn.reshape(bs, ls, d), vsn.reshape(bs, ls, d),
                               pages_t(cache_k), pages_t(cache_v), m * n_pool, page_table, rel_bias)
            hs = _proj_res(os_.reshape(bs * ls, d), w_out, hs)
            ks_l.append(ksn.reshape(bs, ls, MOBA_HEADS, hd))
            vs_l.append(vsn.reshape(bs, ls, MOBA_HEADS, hd))
        last = i == depth - 1
        hp = _ffn(hp, ffn2_norm[i], w2i, w2o, i, final_norm if last else None)
        hs = _ffn(hs, ffn2_norm[i], w2i, w2o, i, final_norm if last else None)
    return (hp.reshape(bp, lp, d), hs.reshape(bs, ls, d), jnp.stack(ret_p), jnp.stack(ret_s),
            jnp.stack(kp_l), jnp.stack(vp_l), jnp.stack(ks_l), jnp.stack(vs_l))
```
